```python
import math
import jax, jax.numpy as jnp
from jax import lax
import numpy as np

D_MODEL = 2048
BATCH = 4
SEQ = 4096
DEPTH = 2

N_A_LAYERS = (DEPTH + 1) // 2
N_B_LAYERS = DEPTH - N_A_LAYERS

RET_HEADS = 8
RET_DK = D_MODEL // RET_HEADS
RET_DV = 2 * RET_DK
RET_VDIM = RET_HEADS * RET_DV
RET_CHUNK = 128
ROPE_BASE = 10000.0
GN_EPS = 1e-5

ATT_HEADS = 16
ATT_DH = D_MODEL // ATT_HEADS
MOBA_BLOCK = 256
MOBA_TOPK = 3
MOBA_QCHUNK = 16

REL_BUCKETS = 32
REL_MAX_DIST = 1024

D_FF = 4 * D_MODEL
PLE_DIM = 256
RMS_EPS = 1e-6
NEG_INF = -1e30

kernel_name = "yoco_retention_moba_hybrid"


def rms_norm(x, g):
    xf = x.astype(jnp.float32)
    y = xf * lax.rsqrt(jnp.mean(xf * xf, axis=-1, keepdims=True) + RMS_EPS)
    return (y * g.astype(jnp.float32)).astype(x.dtype)


def rotary(x, pos):
    half = x.shape[-1] // 2
    inv = ROPE_BASE ** (-jnp.arange(half, dtype=jnp.float32) / half)
    ang = pos.astype(jnp.float32)[:, None] * inv[None, :]
    cos = jnp.cos(ang)[None, :, None, :]
    sin = jnp.sin(ang)[None, :, None, :]
    x1, x2 = x[..., :half], x[..., half:]
    return jnp.concatenate([x1 * cos - x2 * sin, x1 * sin + x2 * cos], axis=-1)


def retention(xn, w_in, w_out):
    B, S, _ = xn.shape
    C = RET_CHUNK
    nC = S // C
    proj = xn @ w_in
    q, k, v, g = jnp.split(proj, [D_MODEL, 2 * D_MODEL, 2 * D_MODEL + RET_VDIM], axis=-1)
    pos = jnp.arange(S)
    q = rotary(q.reshape(B, S, RET_HEADS, RET_DK).astype(jnp.float32), pos)
    k = rotary(k.reshape(B, S, RET_HEADS, RET_DK).astype(jnp.float32), pos) * (RET_DK ** -0.5)
    v = v.reshape(B, S, RET_HEADS, RET_DV).astype(jnp.float32)

    log_gamma = jnp.log1p(-jnp.exp2(-5.0 - jnp.arange(RET_HEADS, dtype=jnp.float32)))
    idx = jnp.arange(C)
    diff = idx[:, None] - idx[None, :]
    decay_mat = jnp.where(diff[None] >= 0,
                          jnp.exp(jnp.maximum(diff, 0)[None].astype(jnp.float32) * log_gamma[:, None, None]),
                          0.0)
    zeta = jnp.exp((C - 1 - idx)[None].astype(jnp.float32) * log_gamma[:, None])
    xi = jnp.exp((idx + 1)[None].astype(jnp.float32) * log_gamma[:, None])
    chunk_decay = jnp.exp(C * log_gamma)

    def to_chunks(t):
        return t.reshape(B, nC, C, t.shape[2], t.shape[3]).transpose(1, 0, 3, 2, 4)

    qc, kc, vc = to_chunks(q), to_chunks(k), to_chunks(v)
    scores = jnp.einsum('nbhqd,nbhkd->nbhqk', qc, kc) * decay_mat[None, None]
    intra = jnp.einsum('nbhqk,nbhke->nbhqe', scores, vc)

    def step(state, inp):
        qn, kn, vn = inp
        out = jnp.einsum('bhqd,bhde->bhqe', qn, state) * xi[None, :, :, None]
        state = state * chunk_decay[None, :, None, None] + \
            jnp.einsum('bhkd,bhke->bhde', kn * zeta[None, :, :, None], vn)
        return state, out

    state0 = jnp.zeros((B, RET_HEADS, RET_DK, RET_DV), jnp.float32)
    _, inter = lax.scan(step, state0, (qc, kc, vc))
    y = (intra + inter).transpose(1, 0, 3, 2, 4).reshape(B, S, RET_HEADS, RET_DV)
    mu = jnp.mean(y, axis=-1, keepdims=True)
    var = jnp.mean(jnp.square(y - mu), axis=-1, keepdims=True)
    y = ((y - mu) * lax.rsqrt(var + GN_EPS)).reshape(B, S, RET_VDIM)
    out = jax.nn.silu(g.astype(jnp.float32)) * y
    return out.astype(xn.dtype) @ w_out


def t5_bucket(dist):
    n = jnp.maximum(dist, 0)
    max_exact = REL_BUCKETS // 2
    nf = jnp.maximum(n, max_exact).astype(jnp.float32)
    large = max_exact + (jnp.log(nf / max_exact) / math.log(REL_MAX_DIST / max_exact)
                         * (REL_BUCKETS - max_exact)).astype(jnp.int32)
    large = jnp.minimum(large, REL_BUCKETS - 1)
    return jnp.where(n < max_exact, n, large)


def shared_kv(h, g_kv, w_kv):
    B, S, _ = h.shape
    nblk = -(-S // MOBA_BLOCK)
    s_pad = nblk * MOBA_BLOCK
    kv = rms_norm(h, g_kv) @ w_kv
    k, v = jnp.split(kv, 2, axis=-1)

    def blocks(t):
        t = t.reshape(B, S, ATT_HEADS, ATT_DH).transpose(0, 2, 1, 3)
        t = jnp.pad(t, ((0, 0), (0, 0), (0, s_pad - S), (0, 0)))
        return t.reshape(B, ATT_HEADS, nblk, MOBA_BLOCK, ATT_DH)

    k_blocks, v_blocks = blocks(k), blocks(v)
    k_means = jnp.mean(k_blocks.astype(jnp.float32), axis=3)
    return k_blocks, v_blocks, k_means


def moba_attention(xn, w_q, w_o, k_blocks, v_blocks, k_means, rel_bias):
    B, S, _ = xn.shape
    H, dh, BLK, QC = ATT_HEADS, ATT_DH, MOBA_BLOCK, MOBA_QCHUNK
    nblk = k_blocks.shape[2]
    s_pad = nblk * BLK
    nqc = s_pad // QC
    topk = min(MOBA_TOPK, nblk)
    scale = dh ** -0.5

    q = (xn @ w_q).reshape(B, S, H, dh).transpose(0, 2, 1, 3)
    q = jnp.pad(q, ((0, 0), (0, 0), (0, s_pad - S), (0, 0)))
    q_blk = jnp.arange(s_pad) // BLK
    gate = jnp.einsum('bhsd,bhnd->bhsn', q, k_means, preferred_element_type=jnp.float32)
    past = jnp.arange(nblk)[None, :] < q_blk[:, None]
    gate = jnp.where(past[None, None], gate, NEG_INF)
    _, sel = lax.top_k(gate, topk)
    valid = sel < q_blk[None, None, :, None]

    def by_chunk(t):
        t = t.reshape((B, H, nqc, QC) + t.shape[3:])
        t = jnp.moveaxis(t, 2, 1)
        return t.reshape((B * nqc, H, QC) + t.shape[4:])

    qx, selx, validx = by_chunk(q), by_chunk(sel), by_chunk(valid)
    bidx = jnp.repeat(jnp.arange(B, dtype=jnp.int32), nqc)
    cidx = jnp.tile(jnp.arange(nqc, dtype=jnp.int32), B)
    h_ar = jnp.arange(H)
    bias_t = rel_bias.T

    def chunk_attn(args):
        qc, sc, vc, b, c = args
        qpos = c * QC + jnp.arange(QC)
        blk = (c * QC) // BLK
        k_own = lax.dynamic_slice(k_blocks, (b, 0, blk, 0, 0), (1, H, 1, BLK, dh)).reshape(H, BLK, dh)
        v_own = lax.dynamic_slice(v_blocks, (b, 0, blk, 0, 0), (1, H, 1, BLK, dh)).reshape(H, BLK, dh)
        k_sel = k_blocks[b, h_ar[:, None, None], sc]
        v_sel = v_blocks[b, h_ar[:, None, None], sc]

        dist_own = qpos[:, None] - (blk * BLK + jnp.arange(BLK))[None, :]
        l_own = jnp.einsum('hqd,hkd->hqk', qc, k_own, preferred_element_type=jnp.float32) * scale
        l_own = jnp.where(dist_own[None] >= 0, l_own + bias_t[:, t5_bucket(dist_own)], NEG_INF)

        dist_sel = qpos[None, :, None, None] - (sc[..., None] * BLK + jnp.arange(BLK))
        l_sel = jnp.einsum('hqd,hqjkd->hqjk', qc, k_sel, preferred_element_type=jnp.float32) * scale
        l_sel = l_sel + bias_t[h_ar[:, None, None, None], t5_bucket(dist_sel)]
        l_sel = jnp.where(vc[..., None], l_sel, NEG_INF)

        logits = jnp.concatenate([l_own, l_sel.reshape(H, QC, topk * BLK)], axis=-1)
        probs = jax.nn.softmax(logits, axis=-1).astype(v_own.dtype)
        p_own = probs[..., :BLK]
        p_sel = probs[..., BLK:].reshape(H, QC, topk, BLK)
        return jnp.einsum('hqk,hkd->hqd', p_own, v_own) + \
            jnp.einsum('hqjk,hqjkd->hqd', p_sel, v_sel)

    out = lax.map(chunk_attn, (qx, selx, validx, bidx, cidx))
    out = out.reshape(B, nqc, H, QC, dh).transpose(0, 1, 3, 2, 4).reshape(B, s_pad, H * dh)
    return out[:, :S] @ w_o


def sq_relu_mlp(xn, w_up, w_down):
    return jnp.square(jax.nn.relu(xn @ w_up)) @ w_down


def per_layer_embed(h, p_i, g, w_up, w_gate):
    gate = jax.nn.sigmoid(rms_norm(h, g) @ w_gate)
    return (p_i @ w_up) * gate


def setup_inputs(seed: int = 0) -> dict:
    key = jax.random.key(seed)
    ks = jax.random.split(key, 20)
    f32 = jnp.float32

    def normal(k, shape, fan_in):
        return jax.random.normal(k, shape, f32) * (fan_in ** -0.5)

    def gain(k, shape):
        return 1.0 + 0.02 * jax.random.normal(k, shape, f32)

    ret_cols = 2 * D_MODEL + 2 * RET_VDIM
    return {
        "x": jax.random.normal(ks[0], (BATCH, SEQ, D_MODEL), f32),
        "p": jax.random.normal(ks[1], (DEPTH, BATCH, SEQ, PLE_DIM), f32),
        "ret_norm_g": gain(ks[2], (N_A_LAYERS, D_MODEL)),
        "ret_w_in": normal(ks[3], (N_A_LAYERS, D_MODEL, ret_cols), D_MODEL),
        "ret_w_out": normal(ks[4], (N_A_LAYERS, RET_VDIM, D_MODEL), RET_VDIM),
        "kv_norm_g": gain(ks[5], (D_MODEL,)),
        "w_kv": normal(ks[6], (D_MODEL, 2 * D_MODEL), D_MODEL),
        "att_norm_g": gain(ks[7], (N_B_LAYERS, D_MODEL)),
        "att_w_q": normal(ks[8], (N_B_LAYERS, D_MODEL, D_MODEL), D_MODEL),
        "att_w_o": normal(ks[9], (N_B_LAYERS, D_MODEL, D_MODEL), D_MODEL),
        "rel_bias": 0.5 * jax.random.normal(ks[10], (REL_BUCKETS, ATT_HEADS), f32),
        "mlp_norm_g": gain(ks[11], (DEPTH, D_MODEL)),
        "mlp_w_up": normal(ks[12], (DEPTH, D_MODEL, D_FF), D_MODEL),
        "mlp_w_down": normal(ks[13], (DEPTH, D_FF, D_MODEL), D_FF),
        "ple_norm_g": gain(ks[14], (DEPTH, D_MODEL)),
        "ple_w_up": normal(ks[15], (DEPTH, PLE_DIM, D_MODEL), PLE_DIM),
        "ple_w_gate": normal(ks[16], (DEPTH, D_MODEL, D_MODEL), D_MODEL),
        "final_norm_g": gain(ks[17], (D_MODEL,)),
    }


def reference(x, p, ret_norm_g, ret_w_in, ret_w_out, kv_norm_g, w_kv, att_norm_g, att_w_q,
              att_w_o, rel_bias, mlp_norm_g, mlp_w_up, mlp_w_down, ple_norm_g, ple_w_up,
              ple_w_gate, final_norm_g):
    h = x
    k_blocks = v_blocks = k_means = None
    for i in range(DEPTH):
        if i < N_A_LAYERS:
            h = h + retention(rms_norm(h, ret_norm_g[i]), ret_w_in[i], ret_w_out[i])
        else:
            j = i - N_A_LAYERS
            h = h + moba_attention(rms_norm(h, att_norm_g[j]), att_w_q[j], att_w_o[j],
                                   k_blocks, v_blocks, k_means, rel_bias)
        h = h + sq_relu_mlp(rms_norm(h, mlp_norm_g[i]), mlp_w_up[i], mlp_w_down[i])
        h = h + per_layer_embed(h, p[i], ple_norm_g[i], ple_w_up[i], ple_w_gate[i])
        if i == N_A_LAYERS - 1:
            k_blocks, v_blocks, k_means = shared_kv(h, kv_norm_g, w_kv)
    return rms_norm(h, final_norm_g)
```

```python
import functools
import math

import jax
import jax.numpy as jnp
from jax import lax
from jax.experimental import pallas as pl
from jax.experimental.pallas import tpu as pltpu

_F32 = jnp.float32
_BF16 = jnp.bfloat16

_RET_HEADS = 8
_RET_CHUNK = 128
_ROPE_BASE = 10000.0
_GN_EPS = 1e-5
_ATT_HEADS = 16
_MOBA_BLOCK = 256
_MOBA_TOPK = 3
_REL_BUCKETS = 32
_REL_MAX_DIST = 1024
_RMS_EPS = 1e-6
_NEG_INF = -1e30

_V7X_VMEM_BUDGET = 60 * 1024 * 1024
_NORM_ROWS = 32


def _vmem_limit(block_bytes, scratch_bytes=0, temp_bytes=0):
    need = 2 * block_bytes + scratch_bytes + temp_bytes + (4 << 20)
    return int(min(max(need, 16 << 20), _V7X_VMEM_BUDGET))


def _nbytes(shape, dtype):
    return math.prod(shape) * jnp.dtype(dtype).itemsize


def _rms_rows(x_ref, g_ref, xn_ref):
    g = g_ref[...]

    def chunk(c, carry):
        r = pl.multiple_of(c * _NORM_ROWS, _NORM_ROWS)
        x = x_ref[pl.ds(r, _NORM_ROWS), :].astype(_F32)
        ms = jnp.mean(x * x, axis=-1, keepdims=True)
        xn_ref[pl.ds(r, _NORM_ROWS), :] = ((x * lax.rsqrt(ms + _RMS_EPS)) * g).astype(_BF16)
        return carry

    lax.fori_loop(0, x_ref.shape[0] // _NORM_ROWS, chunk, 0)


def _mm_body(*refs, has_norm, n_extra, n_out, epilogue):
    x_ref = refs[0]
    pos = 1
    g_ref = None
    if has_norm:
        g_ref = refs[pos]
        pos += 1
    w_ref = refs[pos]
    pos += 1
    extra_refs = refs[pos:pos + n_extra]
    pos += n_extra
    out_refs = refs[pos:pos + n_out]
    pos += n_out
    i = pl.program_id(0)
    j = pl.program_id(1)
    if has_norm:
        xn_ref = refs[pos]

        @pl.when(j == 0)
        def _():
            _rms_rows(x_ref, g_ref, xn_ref)

        lhs = xn_ref[...]
    else:
        lhs = x_ref[...]
    acc = jnp.dot(lhs, w_ref[...], preferred_element_type=_F32)
    epilogue(acc, i, j, extra_refs, out_refs)


def _fused_matmul(x, w, *, bm, bn, n_col_blocks, w_col_block_off=0, gain=None, extras=(),
                  extra_specs=(), out_shapes, out_specs, epilogue, name):
    m, k = x.shape
    assert m % bm == 0 and w.shape[0] == k and w.shape[1] % bn == 0
    has_norm = gain is not None
    in_specs = [pl.BlockSpec((bm, k), lambda i, j: (i, 0))]
    args = [x]
    if has_norm:
        in_specs.append(pl.BlockSpec((1, k), lambda i, j: (0, 0)))
        args.append(gain.reshape(1, k).astype(_F32))
    in_specs.append(pl.BlockSpec((k, bn), lambda i, j: (0, j + w_col_block_off)))
    args.append(w)
    in_specs.extend(extra_specs)
    args.extend(extras)
    scratch = [pltpu.VMEM((bm, k), _BF16)] if has_norm else []
    block_bytes = _nbytes((bm, k), x.dtype) + _nbytes((k, bn), w.dtype)
    for spec, arr in zip(extra_specs, extras):
        block_bytes += _nbytes([d for d in spec.block_shape if d is not None], arr.dtype)
    for spec, sh in zip(out_specs, out_shapes):
        block_bytes += _nbytes([d for d in spec.block_shape if d is not None], sh.dtype)
    scratch_bytes = _nbytes((bm, k), _BF16) if has_norm else 0
    body = functools.partial(_mm_body, has_norm=has_norm, n_extra=len(extras),
                             n_out=len(out_shapes), epilogue=epilogue)
    return pl.pallas_call(
        body,
        grid=(m // bm, n_col_blocks),
        in_specs=in_specs,
        out_specs=list(out_specs),
        out_shape=list(out_shapes),
        scratch_shapes=scratch,
        compiler_params=pltpu.CompilerParams(
            dimension_semantics=("parallel", "arbitrary"),
            vmem_limit_bytes=_vmem_limit(block_bytes, scratch_bytes, 3 * bm * bn * 4)),
        name=name,
    )(*args)


def _retention_proj(h, gain, w_in, cos, sin, seq):
    t, d = h.shape
    n_total = w_in.shape[1]
    bm, bn = 1024, 1024
    dk = d // _RET_HEADS
    half = dk // 2
    n_rot_blocks = (2 * d) // bn
    n_q_blocks = d // bn
    k_scale = dk ** -0.5

    def epilogue(acc, i, j, ex, outs):
        o_ref = outs[0]

        @pl.when(j < n_rot_blocks)
        def _():
            cos_b = ex[0][...]
            sin_b = ex[1][...]
            scale = jnp.where(j < n_q_blocks, 1.0, k_scale).astype(_F32)
            for c in range(bn // dk):
                x1 = acc[:, c * dk:c * dk + half]
                x2 = acc[:, c * dk + half:(c + 1) * dk]
                o_ref[:, c * dk:c * dk + half] = ((x1 * cos_b - x2 * sin_b) * scale).astype(_BF16)
                o_ref[:, c * dk + half:(c + 1) * dk] = ((x1 * sin_b + x2 * cos_b) * scale).astype(_BF16)

        @pl.when(j >= n_rot_blocks)
        def _():
            o_ref[...] = acc.astype(_BF16)

    pos_blocks = seq // bm
    tab_spec = pl.BlockSpec((bm, half), lambda i, j: (i % pos_blocks, 0))
    return _fused_matmul(
        h, w_in, bm=bm, bn=bn, n_col_blocks=n_total // bn, gain=gain,
        extras=(cos, sin), extra_specs=(tab_spec, tab_spec),
        out_shapes=[jax.ShapeDtypeStruct((t, n_total), _BF16)],
        out_specs=[pl.BlockSpec((bm, bn), lambda i, j: (i, j))],
        epilogue=epilogue, name="ret_proj")[0]


def _matmul_residual(x, w, res, *, bm, bn, name):
    t = x.shape[0]
    n = w.shape[1]

    def epilogue(acc, i, j, ex, outs):
        outs[0][...] = ex[0][...] + acc

    spec = pl.BlockSpec((bm, bn), lambda i, j: (i, j))
    return _fused_matmul(
        x, w, bm=bm, bn=bn, n_col_blocks=n // bn, extras=(res,), extra_specs=(spec,),
        out_shapes=[jax.ShapeDtypeStruct((t, n), _F32)], out_specs=[spec],
        epilogue=epilogue, name=name)[0]


def _norm_matmul(h, gain, w, *, bm, bn, act, name):
    t = h.shape[0]
    n = w.shape[1]

    def epilogue(acc, i, j, ex, outs):
        outs[0][...] = act(acc).astype(_BF16)

    return _fused_matmul(
        h, w, bm=bm, bn=bn, n_col_blocks=n // bn, gain=gain,
        out_shapes=[jax.ShapeDtypeStruct((t, n), _BF16)],
        out_specs=[pl.BlockSpec((bm, bn), lambda i, j: (i, j))],
        epilogue=epilogue, name=name)[0]


def _sq_relu(a):
    r = jnp.maximum(a, 0.0)
    return r * r


def _identity(a):
    return a


def _mlp(h, gain, w_up, w_down, name):
    u = _norm_matmul(h, gain, w_up, bm=1024, bn=1024, act=_sq_relu, name=name + "_up")
    return _matmul_residual(u, w_down, h, bm=512, bn=512, name=name + "_down")


def _per_layer_embed(h, p_i, gain, w_up, w_gate, name):
    t, d = h.shape
    pd = p_i.shape[1]
    bm, bn = 1024, 1024

    def epilogue(acc, i, j, ex, outs):
        p_ref, wup_ref, res_ref = ex
        up = jnp.dot(p_ref[...].astype(_BF16), wup_ref[...], preferred_element_type=_F32)
        outs[0][...] = res_ref[...] + up * jax.nn.sigmoid(acc)

    spec = pl.BlockSpec((bm, bn), lambda i, j: (i, j))
    return _fused_matmul(
        h, w_gate, bm=bm, bn=bn, n_col_blocks=d // bn, gain=gain,
        extras=(p_i, w_up, h),
        extra_specs=(pl.BlockSpec((bm, pd), lambda i, j: (i, 0)),
                     pl.BlockSpec((pd, bn), lambda i, j: (0, j)), spec),
        out_shapes=[jax.ShapeDtypeStruct((t, d), _F32)], out_specs=[spec],
        epilogue=epilogue, name=name)[0]


def _shared_k(h, gain, w_kv):
    t, d = h.shape
    bm, bn = 1024, 1024
    per = bm // _MOBA_BLOCK

    def epilogue(acc, i, j, ex, outs):
        outs[0][...] = acc.astype(_BF16)
        for r in range(per):
            blk = acc[r * _MOBA_BLOCK:(r + 1) * _MOBA_BLOCK, :]
            outs[1][0, r:r + 1, :] = jnp.mean(blk, axis=0, keepdims=True)

    k, km = _fused_matmul(
        h, w_kv, bm=bm, bn=bn, n_col_blocks=d // bn, gain=gain,
        out_shapes=[jax.ShapeDtypeStruct((t, d), _BF16),
                    jax.ShapeDtypeStruct((t // bm, per, d), _F32)],
        out_specs=[pl.BlockSpec((bm, bn), lambda i, j: (i, j)),
                   pl.BlockSpec((1, per, bn), lambda i, j: (i, 0, j))],
        epilogue=epilogue, name="shared_k")
    return k, km.reshape(t // _MOBA_BLOCK, d)


def _shared_v(h, gain, w_kv):
    t, d = h.shape
    bm, bn = 1024, 1024

    def epilogue(acc, i, j, ex, outs):
        outs[0][...] = acc.astype(_BF16)

    return _fused_matmul(
        h, w_kv, bm=bm, bn=bn, n_col_blocks=d // bn, w_col_block_off=d // bn, gain=gain,
        out_shapes=[jax.ShapeDtypeStruct((t, d), _BF16)],
        out_specs=[pl.BlockSpec((bm, bn), lambda i, j: (i, j))],
        epilogue=epilogue, name="shared_v")[0]


def _ret_body(lg_ref, q_ref, k_ref, v_ref, g_ref, o_ref, state_ref, *, chunk, n_chunks):
    hd = pl.program_id(1)

    @pl.when(pl.program_id(2) == 0)
    def _():
        state_ref[...] = jnp.zeros_like(state_ref)

    lg = lg_ref[hd]
    ri = lax.broadcasted_iota(jnp.int32, (chunk, chunk), 0)
    ci = lax.broadcasted_iota(jnp.int32, (chunk, chunk), 1)
    diff = ri - ci
    decay = jnp.where(diff >= 0, jnp.exp(jnp.maximum(diff, 0).astype(_F32) * lg), 0.0)
    r1 = lax.broadcasted_iota(jnp.int32, (chunk, 1), 0)
    xi = jnp.exp((r1 + 1).astype(_F32) * lg)
    zeta = jnp.exp((chunk - 1 - r1).astype(_F32) * lg)
    chunk_decay = jnp.exp(jnp.full((1, 1), chunk, _F32) * lg)

    def step(c, carry):
        r = pl.multiple_of(c * chunk, chunk)
        qc = q_ref[pl.ds(r, chunk), :]
        kc = k_ref[pl.ds(r, chunk), :]
        vc = v_ref[pl.ds(r, chunk), :]
        scores = lax.dot_general(qc, kc, (((1,), (1,)), ((), ())),
                                 preferred_element_type=_F32) * decay
        intra = jnp.dot(scores.astype(_BF16), vc, preferred_element_type=_F32)
        state = state_ref[...]
        inter = jnp.dot(qc, state.astype(_BF16), preferred_element_type=_F32) * xi
        kz_t = jnp.transpose(kc.astype(_F32) * zeta).astype(_BF16)
        state_ref[...] = state * chunk_decay + jnp.dot(kz_t, vc, preferred_element_type=_F32)
        y = intra + inter
        mu = jnp.mean(y, axis=-1, keepdims=True)
        yc = y - mu
        var = jnp.mean(yc * yc, axis=-1, keepdims=True)
        yn = yc * lax.rsqrt(var + _GN_EPS)
        gate = g_ref[pl.ds(r, chunk), :].astype(_F32)
        o_ref[pl.ds(r, chunk), :] = (gate * jax.nn.sigmoid(gate) * yn).astype(_BF16)
        return carry

    lax.fori_loop(0, n_chunks, step, 0)


def _retention_core(proj, batch, seq, d):
    t = proj.shape[0]
    dk = d // _RET_HEADS
    dv = 2 * dk
    vdim = _RET_HEADS * dv
    tb = 1024
    nt = seq // tb
    log_gamma = jnp.log1p(-jnp.exp2(-5.0 - jnp.arange(_RET_HEADS, dtype=_F32)))
    qk_blocks = d // dk
    v_off = (2 * d) // dv
    g_off = v_off + _RET_HEADS

    def row(b, h, s):
        return b * nt + s

    block_bytes = 2 * _nbytes((tb, dk), _BF16) + 3 * _nbytes((tb, dv), _BF16)
    return pl.pallas_call(
        functools.partial(_ret_body, chunk=_RET_CHUNK, n_chunks=tb // _RET_CHUNK),
        grid=(batch, _RET_HEADS, nt),
        in_specs=[
            pl.BlockSpec(memory_space=pltpu.SMEM),
            pl.BlockSpec((tb, dk), lambda b, h, s: (row(b, h, s), h)),
            pl.BlockSpec((tb, dk), lambda b, h, s: (row(b, h, s), qk_blocks + h)),
            pl.BlockSpec((tb, dv), lambda b, h, s: (row(b, h, s), v_off + h)),
            pl.BlockSpec((tb, dv), lambda b, h, s: (row(b, h, s), g_off + h)),
        ],
        out_specs=pl.BlockSpec((tb, dv), lambda b, h, s: (row(b, h, s), h)),
        out_shape=jax.ShapeDtypeStruct((t, vdim), _BF16),
        scratch_shapes=[pltpu.VMEM((dk, dv), _F32)],
        compiler_params=pltpu.CompilerParams(
            dimension_semantics=("parallel", "parallel", "arbitrary"),
            vmem_limit_bytes=_vmem_limit(block_bytes, _nbytes((dk, dv), _F32), 8 << 20)),
        name="retention_core",
    )(log_gamma, proj, proj, proj, proj)


def _t5_thresholds():
    max_exact = _REL_BUCKETS // 2
    n_log = _REL_BUCKETS - max_exact
    ratio = _REL_MAX_DIST / max_exact
    thr = list(range(max_exact + 1))
    for kk in range(1, n_log):
        guess = int(math.floor(max_exact * ratio ** (kk / n_log)))
        n = max(guess - 2, max_exact)
        while not (n / max_exact) ** n_log >= ratio ** kk:
            n += 1
        thr.append(n)
    return thr


def _moba_body(rb_ref, q_ref, k_ref, v_ref, km_ref, o_ref, bias_ref, m_ref, l_ref, acc_ref, *,
               blk, nblk, n_tiles, thresholds, scale, topk):
    hd = pl.program_id(0)
    qb = pl.program_id(2)
    ri = lax.broadcasted_iota(jnp.int32, (blk, blk), 0)
    ci = lax.broadcasted_iota(jnp.int32, (blk, blk), 1)

    @pl.when((pl.program_id(1) == 0) & (qb == 0))
    def _():
        def build(dd, carry):
            dist = ri - ci + dd * blk
            val = jnp.full((blk, blk), rb_ref[0, hd], _F32)
            for b in range(1, _REL_BUCKETS):
                val = jnp.where(dist >= thresholds[b], rb_ref[b, hd], val)
            bias_ref[dd] = val
            return carry

        lax.fori_loop(0, n_tiles, build, 0)
        bias_ref[n_tiles] = jnp.full((blk, blk), rb_ref[_REL_BUCKETS - 1, hd], _F32)

    q = q_ref[...]
    km = km_ref[...].astype(_BF16)
    gate = lax.dot_general(q, km, (((1,), (1,)), ((), ())), preferred_element_type=_F32)
    col = lax.broadcasted_iota(jnp.int32, (blk, nblk), 1)
    past = col < qb
    g = jnp.where(past, gate, _NEG_INF)
    sel = jnp.zeros((blk, nblk), jnp.bool_)
    for _ in range(topk):
        mx = jnp.max(g, axis=1, keepdims=True)
        idx = jnp.min(jnp.where(g == mx, col, nblk), axis=1, keepdims=True)
        pick = col == idx
        sel = jnp.logical_or(sel, pick)
        g = jnp.where(pick, -jnp.inf, g)
    pen = jnp.where(jnp.logical_and(sel, past), 0.0, _NEG_INF).astype(_F32)

    r0 = pl.multiple_of(qb * blk, blk)
    s = lax.dot_general(q, k_ref[pl.ds(r0, blk), :], (((1,), (1,)), ((), ())),
                        preferred_element_type=_F32) * scale + bias_ref[0]
    s = jnp.where(ri >= ci, s, _NEG_INF)
    m0 = jnp.max(s, axis=1, keepdims=True)
    p = jnp.exp(s - m0)
    m_ref[...] = m0
    l_ref[...] = jnp.sum(p, axis=1, keepdims=True)
    acc_ref[...] = jnp.dot(p.astype(_BF16), v_ref[pl.ds(r0, blk), :], preferred_element_type=_F32)

    def past_block(n, carry):
        rn = pl.multiple_of(n * blk, blk)
        tile = bias_ref[jnp.minimum(qb - n, n_tiles)]
        pen_n = jnp.max(jnp.where(col == n, pen, _NEG_INF), axis=1, keepdims=True)
        sn = lax.dot_general(q, k_ref[pl.ds(rn, blk), :], (((1,), (1,)), ((), ())),
                             preferred_element_type=_F32) * scale + tile + pen_n
        m_old = m_ref[...]
        m_new = jnp.maximum(m_old, jnp.max(sn, axis=1, keepdims=True))
        alpha = jnp.exp(m_old - m_new)
        pn = jnp.exp(sn - m_new)
        l_ref[...] = alpha * l_ref[...] + jnp.sum(pn, axis=1, keepdims=True)
        acc_ref[...] = alpha * acc_ref[...] + jnp.dot(pn.astype(_BF16), v_ref[pl.ds(rn, blk), :],
                                                      preferred_element_type=_F32)
        m_ref[...] = m_new
        return carry

    lax.fori_loop(0, qb, past_block, 0)
    o_ref[...] = (acc_ref[...] / l_ref[...]).astype(_BF16)


def _moba_attention(q, k, v, k_means, rel_bias, batch, seq):
    t, d = q.shape
    dh = d // _ATT_HEADS
    blk = _MOBA_BLOCK
    nblk = seq // blk
    thresholds = _t5_thresholds()
    n_tiles = 0
    while n_tiles * blk - (blk - 1) < thresholds[_REL_BUCKETS - 1]:
        n_tiles += 1
    n_tiles = min(n_tiles, nblk)
    block_bytes = (2 * _nbytes((blk, dh), _BF16) + 2 * _nbytes((seq, dh), _BF16)
                   + _nbytes((nblk, dh), _F32))
    scratch_bytes = (n_tiles + 1) * blk * blk * 4 + 3 * blk * 128 * 4
    return pl.pallas_call(
        functools.partial(_moba_body, blk=blk, nblk=nblk, n_tiles=n_tiles, thresholds=thresholds,
                          scale=dh ** -0.5, topk=min(_MOBA_TOPK, nblk)),
        grid=(_ATT_HEADS, batch, nblk),
        in_specs=[
            pl.BlockSpec(memory_space=pltpu.SMEM),
            pl.BlockSpec((blk, dh), lambda h, b, s: (b * nblk + s, h)),
            pl.BlockSpec((seq, dh), lambda h, b, s: (b, h)),
            pl.BlockSpec((seq, dh), lambda h, b, s: (b, h)),
            pl.BlockSpec((nblk, dh), lambda h, b, s: (b, h)),
        ],
        out_specs=pl.BlockSpec((blk, dh), lambda h, b, s: (b * nblk + s, h)),
        out_shape=jax.ShapeDtypeStruct((t, d), _BF16),
        scratch_shapes=[
            pltpu.VMEM((n_tiles + 1, blk, blk), _F32),
            pltpu.VMEM((blk, 1), _F32),
            pltpu.VMEM((blk, 1), _F32),
            pltpu.VMEM((blk, dh), _F32),
        ],
        compiler_params=pltpu.CompilerParams(
            dimension_semantics=("arbitrary", "arbitrary", "arbitrary"),
            vmem_limit_bytes=_vmem_limit(block_bytes, scratch_bytes, 8 << 20)),
        name="moba_attention",
    )(rel_bias.astype(_F32), q, k, v, k_means)


def _final_norm_body(x_ref, g_ref, o_ref):
    g = g_ref[...]

    def chunk(c, carry):
        r = pl.multiple_of(c * _NORM_ROWS, _NORM_ROWS)
        x = x_ref[pl.ds(r, _NORM_ROWS), :]
        ms = jnp.mean(x * x, axis=-1, keepdims=True)
        o_ref[pl.ds(r, _NORM_ROWS), :] = (x * lax.rsqrt(ms + _RMS_EPS)) * g
        return carry

    lax.fori_loop(0, x_ref.shape[0] // _NORM_ROWS, chunk, 0)


def _final_norm(h, gain):
    t, d = h.shape
    bm = 512
    return pl.pallas_call(
        _final_norm_body,
        grid=(t // bm,),
        in_specs=[pl.BlockSpec((bm, d), lambda i: (i, 0)), pl.BlockSpec((1, d), lambda i: (0, 0))],
        out_specs=pl.BlockSpec((bm, d), lambda i: (i, 0)),
        out_shape=jax.ShapeDtypeStruct((t, d), _F32),
        compiler_params=pltpu.CompilerParams(
            dimension_semantics=("parallel",),
            vmem_limit_bytes=_vmem_limit(2 * _nbytes((bm, d), _F32))),
        name="final_norm",
    )(h, gain.reshape(1, d).astype(_F32))


def _rotary_tables(seq, half):
    inv = _ROPE_BASE ** (-jnp.arange(half, dtype=_F32) / half)
    ang = jnp.arange(seq).astype(_F32)[:, None] * inv[None, :]
    return jnp.cos(ang), jnp.sin(ang)


def kernel(x, p, ret_norm_g, ret_w_in, ret_w_out, kv_norm_g, w_kv, att_norm_g, att_w_q, att_w_o,
           rel_bias, mlp_norm_g, mlp_w_up, mlp_w_down, ple_norm_g, ple_w_up, ple_w_gate,
           final_norm_g):
    batch, seq, d = x.shape
    depth = p.shape[0]
    n_ret = ret_w_in.shape[0]
    t = batch * seq
    h = x.reshape(t, d)
    p2 = p.reshape(depth, t, p.shape[-1])
    cos, sin = _rotary_tables(seq, (d // _RET_HEADS) // 2)
    k = v = k_means = None
    for i in range(depth):
        if i < n_ret:
            proj = _retention_proj(h, ret_norm_g[i], ret_w_in[i].astype(_BF16), cos, sin, seq)
            y = _retention_core(proj, batch, seq, d)
            h = _matmul_residual(y, ret_w_out[i].astype(_BF16), h, bm=512, bn=1024, name="ret_out")
        else:
            j = i - n_ret
            q = _norm_matmul(h, att_norm_g[j], att_w_q[j].astype(_BF16), bm=1024, bn=1024,
                             act=_identity, name="att_q")
            o = _moba_attention(q, k, v, k_means, rel_bias, batch, seq)
            h = _matmul_residual(o, att_w_o[j].astype(_BF16), h, bm=1024, bn=1024, name="att_out")
        h = _mlp(h, mlp_norm_g[i], mlp_w_up[i].astype(_BF16), mlp_w_down[i].astype(_BF16),
                 name=f"mlp{i}")
        h = _per_layer_embed(h, p2[i], ple_norm_g[i], ple_w_up[i].astype(_BF16),
                             ple_w_gate[i].astype(_BF16), name=f"ple{i}")
        if i == n_ret - 1:
            w_kv_b = w_kv.astype(_BF16)
            k, k_means = _shared_k(h, kv_norm_g, w_kv_b)
            v = _shared_v(h, kv_norm_g, w_kv_b)
    return _final_norm(h, final_norm_g).reshape(batch, seq, d)
```

```python
import functools
import math

import jax
import jax.numpy as jnp
from jax import lax
from jax.experimental import pallas as pl
from jax.experimental.pallas import tpu as pltpu

_F32 = jnp.float32
_BF16 = jnp.bfloat16

_RET_HEADS = 8
_RET_CHUNK = 128
_ROPE_BASE = 10000.0
_GN_EPS = 1e-5
_ATT_HEADS = 16
_MOBA_BLOCK = 256
_MOBA_TOPK = 3
_REL_BUCKETS = 32
_REL_MAX_DIST = 1024
_RMS_EPS = 1e-6
_NEG_INF = -1e30

_V7X_VMEM_BUDGET = 60 * 1024 * 1024
_NORM_ROWS = 32


def _vmem_limit(block_bytes, scratch_bytes=0, temp_bytes=0):
    need = 2 * block_bytes + scratch_bytes + temp_bytes + (4 << 20)
    return int(min(max(need, 16 << 20), _V7X_VMEM_BUDGET))


def _nbytes(shape, dtype):
    return math.prod(shape) * jnp.dtype(dtype).itemsize


def _rms_rows(x_ref, g_ref, xn_ref):
    g = g_ref[...]

    def chunk(c, carry):
        r = pl.multiple_of(c * _NORM_ROWS, _NORM_ROWS)
        x = x_ref[pl.ds(r, _NORM_ROWS), :].astype(_F32)
        ms = jnp.mean(x * x, axis=-1, keepdims=True)
        xn_ref[pl.ds(r, _NORM_ROWS), :] = ((x * lax.rsqrt(ms + _RMS_EPS)) * g).astype(_BF16)
        return carry

    lax.fori_loop(0, x_ref.shape[0] // _NORM_ROWS, chunk, 0)


def _mm_body(*refs, has_norm, n_extra, n_out, epilogue):
    x_ref = refs[0]
    pos = 1
    g_ref = None
    if has_norm:
        g_ref = refs[pos]
        pos += 1
    w_ref = refs[pos]
    pos += 1
    extra_refs = refs[pos:pos + n_extra]
    pos += n_extra
    out_refs = refs[pos:pos + n_out]
    pos += n_out
    i = pl.program_id(0)
    j = pl.program_id(1)
    if has_norm:
        xn_ref = refs[pos]

        @pl.when(j == 0)
        def _():
            _rms_rows(x_ref, g_ref, xn_ref)

        lhs = xn_ref[...]
    else:
        lhs = x_ref[...]
    acc = jnp.dot(lhs, w_ref[...], preferred_element_type=_F32)
    epilogue(acc, i, j, extra_refs, out_refs)


def _fused_matmul(x, w, *, bm, bn, n_col_blocks, w_col_block_off=0, gain=None, extras=(),
                  extra_specs=(), out_shapes, out_specs, epilogue, name):
    m, k = x.shape
    assert m % bm == 0 and w.shape[0] == k and w.shape[1] % bn == 0
    has_norm = gain is not None
    in_specs = [pl.BlockSpec((bm, k), lambda i, j: (i, 0))]
    args = [x]
    if has_norm:
        in_specs.append(pl.BlockSpec((1, k), lambda i, j: (0, 0)))
        args.append(gain.reshape(1, k).astype(_F32))
    in_specs.append(pl.BlockSpec((k, bn), lambda i, j: (0, j + w_col_block_off)))
    args.append(w)
    in_specs.extend(extra_specs)
    args.extend(extras)
    scratch = [pltpu.VMEM((bm, k), _BF16)] if has_norm else []
    block_bytes = _nbytes((bm, k), x.dtype) + _nbytes((k, bn), w.dtype)
    for spec, arr in zip(extra_specs, extras):
        block_bytes += _nbytes([d for d in spec.block_shape if d is not None], arr.dtype)
    for spec, sh in zip(out_specs, out_shapes):
        block_bytes += _nbytes([d for d in spec.block_shape if d is not None], sh.dtype)
    scratch_bytes = _nbytes((bm, k), _BF16) if has_norm else 0
    body = functools.partial(_mm_body, has_norm=has_norm, n_extra=len(extras),
                             n_out=len(out_shapes), epilogue=epilogue)
    return pl.pallas_call(
        body,
        grid=(m // bm, n_col_blocks),
        in_specs=in_specs,
        out_specs=list(out_specs),
        out_shape=list(out_shapes),
        scratch_shapes=scratch,
        compiler_params=pltpu.CompilerParams(
            dimension_semantics=("parallel", "arbitrary"),
            vmem_limit_bytes=_vmem_limit(block_bytes, scratch_bytes, 3 * bm * bn * 4)),
        name=name,
    )(*args)


def _retention_proj(h, gain, w_in, cos, sin, seq):
    t, d = h.shape
    n_total = w_in.shape[1]
    bm, bn = 1024, 1024
    dk = d // _RET_HEADS
    half = dk // 2
    n_rot_blocks = (2 * d) // bn
    n_q_blocks = d // bn
    k_scale = dk ** -0.5

    def epilogue(acc, i, j, ex, outs):
        o_ref = outs[0]

        @pl.when(j < n_rot_blocks)
        def _():
            cos_b = ex[0][...]
            sin_b = ex[1][...]
            scale = jnp.where(j < n_q_blocks, 1.0, k_scale).astype(_F32)
            for c in range(bn // dk):
                x1 = acc[:, c * dk:c * dk + half]
                x2 = acc[:, c * dk + half:(c + 1) * dk]
                o_ref[:, c * dk:c * dk + half] = ((x1 * cos_b - x2 * sin_b) * scale).astype(_BF16)
                o_ref[:, c * dk + half:(c + 1) * dk] = ((x1 * sin_b + x2 * cos_b) * scale).astype(_BF16)

        @pl.when(j >= n_rot_blocks)
        def _():
            o_ref[...] = acc.astype(_BF16)

    pos_blocks = seq // bm
    tab_spec = pl.BlockSpec((bm, half), lambda i, j: (i % pos_blocks, 0))
    return _fused_matmul(
        h, w_in, bm=bm, bn=bn, n_col_blocks=n_total // bn, gain=gain,
        extras=(cos, sin), extra_specs=(tab_spec, tab_spec),
        out_shapes=[jax.ShapeDtypeStruct((t, n_total), _BF16)],
        out_specs=[pl.BlockSpec((bm, bn), lambda i, j: (i, j))],
        epilogue=epilogue, name="ret_proj")[0]


def _matmul_residual(x, w, res, *, bm, bn, name):
    t = x.shape[0]
    n = w.shape[1]

    def epilogue(acc, i, j, ex, outs):
        outs[0][...] = ex[0][...] + acc

    spec = pl.BlockSpec((bm, bn), lambda i, j: (i, j))
    return _fused_matmul(
        x, w, bm=bm, bn=bn, n_col_blocks=n // bn, extras=(res,), extra_specs=(spec,),
        out_shapes=[jax.ShapeDtypeStruct((t, n), _F32)], out_specs=[spec],
        epilogue=epilogue, name=name)[0]


def _norm_matmul(h, gain, w, *, bm, bn, act, name):
    t = h.shape[0]
    n = w.shape[1]

    def epilogue(acc, i, j, ex, outs):
        outs[0][...] = act(acc).astype(_BF16)

    return _fused_matmul(
        h, w, bm=bm, bn=bn, n_col_blocks=n // bn, gain=gain,
        out_shapes=[jax.ShapeDtypeStruct((t, n), _BF16)],
        out_specs=[pl.BlockSpec((bm, bn), lambda i, j: (i, j))],
        epilogue=epilogue, name=name)[0]


def _sq_relu(a):
    r = jnp.maximum(a, 0.0)
    return r * r


def _identity(a):
    return a


def _mlp(h, gain, w_up, w_down, name):
    u = _norm_matmul(h, gain, w_up, bm=1024, bn=1024, act=_sq_relu, name=name + "_up")
    return _matmul_residual(u, w_down, h, bm=512, bn=512, name=name + "_down")


def _per_layer_embed(h, p_i, gain, w_up, w_gate, name):
    t, d = h.shape
    pd = p_i.shape[1]
    bm, bn = 1024, 1024

    def epilogue(acc, i, j, ex, outs):
        p_ref, wup_ref, res_ref = ex
        up = jnp.dot(p_ref[...].astype(_BF16), wup_ref[...], preferred_element_type=_F32)
        outs[0][...] = res_ref[...] + up * jax.nn.sigmoid(acc)

    spec = pl.BlockSpec((bm, bn), lambda i, j: (i, j))
    return _fused_matmul(
        h, w_gate, bm=bm, bn=bn, n_col_blocks=d // bn, gain=gain,
        extras=(p_i, w_up, h),
        extra_specs=(pl.BlockSpec((bm, pd), lambda i, j: (i, 0)),
                     pl.BlockSpec((pd, bn), lambda i, j: (0, j)), spec),
        out_shapes=[jax.ShapeDtypeStruct((t, d), _F32)], out_specs=[spec],
        epilogue=epilogue, name=name)[0]


def _shared_k(h, gain, w_kv):
    t, d = h.shape
    bm, bn = 1024, 1024
    per = bm // _MOBA_BLOCK

    def epilogue(acc, i, j, ex, outs):
        outs[0][...] = acc.astype(_BF16)
        for r in range(per):
            blk = acc[r * _MOBA_BLOCK:(r + 1) * _MOBA_BLOCK, :]
            outs[1][0, r:r + 1, :] = jnp.mean(blk, axis=0, keepdims=True)

    k, km = _fused_matmul(
        h, w_kv, bm=bm, bn=bn, n_col_blocks=d // bn, gain=gain,
        out_shapes=[jax.ShapeDtypeStruct((t, d), _BF16),
                    jax.ShapeDtypeStruct((t // bm, per, d), _F32)],
        out_specs=[pl.BlockSpec((bm, bn), lambda i, j: (i, j)),
                   pl.BlockSpec((1, per, bn), lambda i, j: (i, 0, j))],
        epilogue=epilogue, name="shared_k")
    return k, km.reshape(t // _MOBA_BLOCK, d)


def _shared_v(h, gain, w_kv):
    t, d = h.shape
    bm, bn = 1024, 1024

    def epilogue(acc, i, j, ex, outs):
        outs[0][...] = acc.astype(_BF16)

    return _fused_matmul(
        h, w_kv, bm=bm, bn=bn, n_col_blocks=d // bn, w_col_block_off=d // bn, gain=gain,
        out_shapes=[jax.ShapeDtypeStruct((t, d), _BF16)],
        out_specs=[pl.BlockSpec((bm, bn), lambda i, j: (i, j))],
        epilogue=epilogue, name="shared_v")[0]


def _ret_body(lg_ref, q_ref, k_ref, v_ref, g_ref, o_ref, state_ref, *, chunk, n_chunks):
    hd = pl.program_id(1)

    @pl.when(pl.program_id(2) == 0)
    def _():
        state_ref[...] = jnp.zeros_like(state_ref)

    lg = lg_ref[hd]
    ri = lax.broadcasted_iota(jnp.int32, (chunk, chunk), 0)
    ci = lax.broadcasted_iota(jnp.int32, (chunk, chunk), 1)
    diff = ri - ci
    decay = jnp.where(diff >= 0, jnp.exp(jnp.maximum(diff, 0).astype(_F32) * lg), 0.0)
    r1 = lax.broadcasted_iota(jnp.int32, (chunk, 1), 0)
    xi = jnp.exp((r1 + 1).astype(_F32) * lg)
    zeta = jnp.exp((chunk - 1 - r1).astype(_F32) * lg)
    chunk_decay = jnp.exp(jnp.full((1, 1), chunk, _F32) * lg)

    def step(c, carry):
        r = pl.multiple_of(c * chunk, chunk)
        qc = q_ref[pl.ds(r, chunk), :]
        kc = k_ref[pl.ds(r, chunk), :]
        vc = v_ref[pl.ds(r, chunk), :]
        scores = lax.dot_general(qc, kc, (((1,), (1,)), ((), ())),
                                 preferred_element_type=_F32) * decay
        intra = jnp.dot(scores.astype(_BF16), vc, preferred_element_type=_F32)
        state = state_ref[...]
        inter = jnp.dot(qc, state.astype(_BF16), preferred_element_type=_F32) * xi
        kz_t = jnp.transpose(kc.astype(_F32) * zeta).astype(_BF16)
        state_ref[...] = state * chunk_decay + jnp.dot(kz_t, vc, preferred_element_type=_F32)
        y = intra + inter
        mu = jnp.mean(y, axis=-1, keepdims=True)
        yc = y - mu
        var = jnp.mean(yc * yc, axis=-1, keepdims=True)
        yn = yc * lax.rsqrt(var + _GN_EPS)
        gate = g_ref[pl.ds(r, chunk), :].astype(_F32)
        o_ref[pl.ds(r, chunk), :] = (gate * jax.nn.sigmoid(gate) * yn).astype(_BF16)
        return carry

    lax.fori_loop(0, n_chunks, step, 0)


def _retention_core(proj, batch, seq, d):
    t = proj.shape[0]
    dk = d // _RET_HEADS
    dv = 2 * dk
    vdim = _RET_HEADS * dv
    tb = 1024
    nt = seq // tb
    log_gamma = jnp.log1p(-jnp.exp2(-5.0 - jnp.arange(_RET_HEADS, dtype=_F32)))
    qk_blocks = d // dk
    v_off = (2 * d) // dv
    g_off = v_off + _RET_HEADS

    def row(b, h, s):
        return b * nt + s

    block_bytes = 2 * _nbytes((tb, dk), _BF16) + 3 * _nbytes((tb, dv), _BF16)
    return pl.pallas_call(
        functools.partial(_ret_body, chunk=_RET_CHUNK, n_chunks=tb // _RET_CHUNK),
        grid=(batch, _RET_HEADS, nt),
        in_specs=[
            pl.BlockSpec(memory_space=pltpu.SMEM),
            pl.BlockSpec((tb, dk), lambda b, h, s: (row(b, h, s), h)),
            pl.BlockSpec((tb, dk), lambda b, h, s: (row(b, h, s), qk_blocks + h)),
            pl.BlockSpec((tb, dv), lambda b, h, s: (row(b, h, s), v_off + h)),
            pl.BlockSpec((tb, dv), lambda b, h, s: (row(b, h, s), g_off + h)),
        ],
        out_specs=pl.BlockSpec((tb, dv), lambda b, h, s: (row(b, h, s), h)),
        out_shape=jax.ShapeDtypeStruct((t, vdim), _BF16),
        scratch_shapes=[pltpu.VMEM((dk, dv), _F32)],
        compiler_params=pltpu.CompilerParams(
            dimension_semantics=("parallel", "parallel", "arbitrary"),
            vmem_limit_bytes=_vmem_limit(block_bytes, _nbytes((dk, dv), _F32), 8 << 20)),
        name="retention_core",
    )(log_gamma, proj, proj, proj, proj)


def _t5_thresholds():
    max_exact = _REL_BUCKETS // 2
    n_log = _REL_BUCKETS - max_exact
    ratio = _REL_MAX_DIST / max_exact
    thr = list(range(max_exact + 1))
    for kk in range(1, n_log):
        guess = int(math.floor(max_exact * ratio ** (kk / n_log)))
        n = max(guess - 2, max_exact)
        while not (n / max_exact) ** n_log >= ratio ** kk:
            n += 1
        thr.append(n)
    return thr


def _moba_body(rb_ref, q_ref, k_ref, v_ref, km_ref, o_ref, bias_ref, kaug_ref, *,
               blk, nblk, n_tiles, thresholds, scale, topk, group):
    hd = pl.program_id(0)
    qb = pl.program_id(2)
    dh = q_ref.shape[1]

    @pl.when((pl.program_id(1) == 0) & (qb == 0))
    def _():
        ri = lax.broadcasted_iota(jnp.int32, (blk, blk), 0)
        ci = lax.broadcasted_iota(jnp.int32, (blk, blk), 1)

        def build(dd, carry):
            dist = ri - ci + dd * blk
            val = jnp.full((blk, blk), rb_ref[0, hd], _F32)
            for b in range(1, _REL_BUCKETS):
                val = jnp.where(dist >= thresholds[b], rb_ref[b, hd], val)
            bias_ref[dd] = jnp.where(dist >= 0, val, _NEG_INF)
            return carry

        lax.fori_loop(0, n_tiles, build, 0)
        bias_ref[n_tiles] = jnp.full((blk, blk), rb_ref[_REL_BUCKETS - 1, hd], _F32)

    @pl.when(qb == 0)
    def _():
        kaug_ref[:, 0:dh] = k_ref[...]
        lane = lax.broadcasted_iota(jnp.int32, (blk, dh), 1)
        for n in range(nblk):
            kaug_ref[n * blk:(n + 1) * blk, dh:2 * dh] = jnp.where(lane == n, 1.0, 0.0).astype(_BF16)

    q = q_ref[...]
    km = km_ref[...].astype(_BF16)
    gate = lax.dot_general(km, q, (((1,), (1,)), ((), ())), preferred_element_type=_F32)
    row = lax.broadcasted_iota(jnp.int32, (nblk, blk), 0)
    past = row < qb
    g = jnp.where(past, gate, _NEG_INF)
    sel = jnp.zeros((nblk, blk), jnp.bool_)
    for _ in range(topk):
        mx = jnp.max(g, axis=0, keepdims=True)
        idx = jnp.min(jnp.where(g == mx, row, nblk), axis=0, keepdims=True)
        pick = row == idx
        sel = jnp.logical_or(sel, pick)
        g = jnp.where(pick, -jnp.inf, g)
    attend = jnp.logical_or(jnp.logical_and(sel, past), row == qb)
    pen_t = jnp.where(attend, 0.0, _NEG_INF).astype(_F32)
    pen_t = jnp.concatenate([pen_t, jnp.zeros((dh - nblk, blk), _F32)], axis=0)
    q_aug = jnp.concatenate([q, jnp.transpose(pen_t).astype(_BF16)], axis=1)

    def attend_blocks(nb):
        s = []
        for n in range(nb):
            tile = bias_ref[jnp.clip(qb - n, 0, n_tiles)]
            qk = lax.dot_general(q_aug, kaug_ref[n * blk:(n + 1) * blk, :], (((1,), (1,)), ((), ())),
                                 preferred_element_type=_F32)
            s.append(qk * scale + tile)
        m_el = s[0]
        for n in range(1, nb):
            m_el = jnp.maximum(m_el, s[n])
        m = jnp.max(m_el, axis=1, keepdims=True)
        l_el = None
        acc = None
        for n in range(nb):
            p = jnp.exp(s[n] - m)
            pv = jnp.dot(p.astype(_BF16), v_ref[n * blk:(n + 1) * blk, :], preferred_element_type=_F32)
            l_el = p if l_el is None else l_el + p
            acc = pv if acc is None else acc + pv
        l = jnp.sum(l_el, axis=1, keepdims=True)
        o_ref[...] = (acc / l).astype(_BF16)

    for gi in range(nblk // group):
        @pl.when(jnp.logical_and(qb >= gi * group, qb < (gi + 1) * group))
        def _(gi=gi):
            attend_blocks((gi + 1) * group)


_MOBA_GROUP = 4


def _moba_attention(q, k, v, k_means, rel_bias, batch, seq):
    t, d = q.shape
    dh = d // _ATT_HEADS
    blk = _MOBA_BLOCK
    nblk = seq // blk
    thresholds = _t5_thresholds()
    n_tiles = 0
    while n_tiles * blk - (blk - 1) < thresholds[_REL_BUCKETS - 1]:
        n_tiles += 1
    n_tiles = min(n_tiles, nblk)
    group = math.gcd(_MOBA_GROUP, nblk)
    assert nblk <= dh
    block_bytes = (2 * _nbytes((blk, dh), _BF16) + 2 * _nbytes((seq, dh), _BF16)
                   + _nbytes((nblk, dh), _F32))
    scratch_bytes = (n_tiles + 1) * blk * blk * 4 + _nbytes((seq, 2 * dh), _BF16)
    return pl.pallas_call(
        functools.partial(_moba_body, blk=blk, nblk=nblk, n_tiles=n_tiles, thresholds=thresholds,
                          scale=dh ** -0.5, topk=min(_MOBA_TOPK, nblk), group=group),
        grid=(_ATT_HEADS, batch, nblk),
        in_specs=[
            pl.BlockSpec(memory_space=pltpu.SMEM),
            pl.BlockSpec((blk, dh), lambda h, b, s: (b * nblk + s, h)),
            pl.BlockSpec((seq, dh), lambda h, b, s: (b, h)),
            pl.BlockSpec((seq, dh), lambda h, b, s: (b, h)),
            pl.BlockSpec((nblk, dh), lambda h, b, s: (b, h)),
        ],
        out_specs=pl.BlockSpec((blk, dh), lambda h, b, s: (b * nblk + s, h)),
        out_shape=jax.ShapeDtypeStruct((t, d), _BF16),
        scratch_shapes=[
            pltpu.VMEM((n_tiles + 1, blk, blk), _F32),
            pltpu.VMEM((seq, 2 * dh), _BF16),
        ],
        compiler_params=pltpu.CompilerParams(
            dimension_semantics=("arbitrary", "arbitrary", "arbitrary"),
            vmem_limit_bytes=_vmem_limit(block_bytes, scratch_bytes, 16 << 20)),
        name="moba_attention",
    )(rel_bias.astype(_F32), q, k, v, k_means)


def _final_norm_body(x_ref, g_ref, o_ref):
    g = g_ref[...]

    def chunk(c, carry):
        r = pl.multiple_of(c * _NORM_ROWS, _NORM_ROWS)
        x = x_ref[pl.ds(r, _NORM_ROWS), :]
        ms = jnp.mean(x * x, axis=-1, keepdims=True)
        o_ref[pl.ds(r, _NORM_ROWS), :] = (x * lax.rsqrt(ms + _RMS_EPS)) * g
        return carry

    lax.fori_loop(0, x_ref.shape[0] // _NORM_ROWS, chunk, 0)


def _final_norm(h, gain):
    t, d = h.shape
    bm = 512
    return pl.pallas_call(
        _final_norm_body,
        grid=(t // bm,),
        in_specs=[pl.BlockSpec((bm, d), lambda i: (i, 0)), pl.BlockSpec((1, d), lambda i: (0, 0))],
        out_specs=pl.BlockSpec((bm, d), lambda i: (i, 0)),
        out_shape=jax.ShapeDtypeStruct((t, d), _F32),
        compiler_params=pltpu.CompilerParams(
            dimension_semantics=("parallel",),
            vmem_limit_bytes=_vmem_limit(2 * _nbytes((bm, d), _F32))),
        name="final_norm",
    )(h, gain.reshape(1, d).astype(_F32))


def _rotary_tables(seq, half):
    inv = _ROPE_BASE ** (-jnp.arange(half, dtype=_F32) / half)
    ang = jnp.arange(seq).astype(_F32)[:, None] * inv[None, :]
    return jnp.cos(ang), jnp.sin(ang)


def kernel(x, p, ret_norm_g, ret_w_in, ret_w_out, kv_norm_g, w_kv, att_norm_g, att_w_q, att_w_o,
           rel_bias, mlp_norm_g, mlp_w_up, mlp_w_down, ple_norm_g, ple_w_up, ple_w_gate,
           final_norm_g):
    batch, seq, d = x.shape
    depth = p.shape[0]
    n_ret = ret_w_in.shape[0]
    t = batch * seq
    h = x.reshape(t, d)
    p2 = p.reshape(depth, t, p.shape[-1])
    cos, sin = _rotary_tables(seq, (d // _RET_HEADS) // 2)
    k = v = k_means = None
    for i in range(depth):
        if i < n_ret:
            proj = _retention_proj(h, ret_norm_g[i], ret_w_in[i].astype(_BF16), cos, sin, seq)
            y = _retention_core(proj, batch, seq, d)
            h = _matmul_residual(y, ret_w_out[i].astype(_BF16), h, bm=512, bn=1024, name="ret_out")
        else:
            j = i - n_ret
            q = _norm_matmul(h, att_norm_g[j], att_w_q[j].astype(_BF16), bm=1024, bn=1024,
                             act=_identity, name="att_q")
            o = _moba_attention(q, k, v, k_means, rel_bias, batch, seq)
            h = _matmul_residual(o, att_w_o[j].astype(_BF16), h, bm=1024, bn=1024, name="att_out")
        h = _mlp(h, mlp_norm_g[i], mlp_w_up[i].astype(_BF16), mlp_w_down[i].astype(_BF16),
                 name=f"mlp{i}")
        h = _per_layer_embed(h, p2[i], ple_norm_g[i], ple_w_up[i].astype(_BF16),
                             ple_w_gate[i].astype(_BF16), name=f"ple{i}")
        if i == n_ret - 1:
            w_kv_b = w_kv.astype(_BF16)
            k, k_means = _shared_k(h, kv_norm_g, w_kv_b)
            v = _shared_v(h, kv_norm_g, w_kv_b)
    return _final_norm(h, final_norm_g).reshape(batch, seq, d)
```

```python
import functools
import math

import jax
import jax.numpy as jnp
from jax import lax
from jax.experimental import pallas as pl
from jax.experimental.pallas import tpu as pltpu

_F32 = jnp.float32
_BF16 = jnp.bfloat16

_RET_HEADS = 8
_RET_CHUNK = 128
_ROPE_BASE = 10000.0
_GN_EPS = 1e-5
_ATT_HEADS = 16
_MOBA_BLOCK = 256
_MOBA_TOPK = 3
_REL_BUCKETS = 32
_REL_MAX_DIST = 1024
_RMS_EPS = 1e-6
_NEG_INF = -1e30
_LOG2E = math.log2(math.e)

_V7X_VMEM_BUDGET = 60 * 1024 * 1024
_NORM_ROWS = 32
_NORM_UNROLL = 4


def _vmem_limit(block_bytes, scratch_bytes=0, temp_bytes=0):
    need = 2 * block_bytes + scratch_bytes + temp_bytes + (4 << 20)
    return int(min(max(need, 16 << 20), _V7X_VMEM_BUDGET))


def _nbytes(shape, dtype):
    return math.prod(shape) * jnp.dtype(dtype).itemsize


def _rms_rows(x_ref, g_ref, xn_ref):
    g = g_ref[...]

    def chunk(c, carry):
        r = pl.multiple_of(c * _NORM_ROWS, _NORM_ROWS)
        x = x_ref[pl.ds(r, _NORM_ROWS), :].astype(_F32)
        ms = jnp.mean(x * x, axis=-1, keepdims=True)
        xn_ref[pl.ds(r, _NORM_ROWS), :] = ((x * lax.rsqrt(ms + _RMS_EPS)) * g).astype(_BF16)
        return carry

    lax.fori_loop(0, x_ref.shape[0] // _NORM_ROWS, chunk, 0, unroll=_NORM_UNROLL)


def _mm_body(*refs, has_norm, n_extra, n_out, epilogue):
    x_ref = refs[0]
    pos = 1
    g_ref = None
    if has_norm:
        g_ref = refs[pos]
        pos += 1
    w_ref = refs[pos]
    pos += 1
    extra_refs = refs[pos:pos + n_extra]
    pos += n_extra
    out_refs = refs[pos:pos + n_out]
    pos += n_out
    i = pl.program_id(0)
    j = pl.program_id(1)
    if has_norm:
        xn_ref = refs[pos]

        @pl.when(j == 0)
        def _():
            _rms_rows(x_ref, g_ref, xn_ref)

        lhs = xn_ref[...]
    else:
        lhs = x_ref[...]
    acc = jnp.dot(lhs, w_ref[...], preferred_element_type=_F32)
    epilogue(acc, i, j, extra_refs, out_refs)


def _fused_matmul(x, w, *, bm, bn, n_col_blocks, w_col_block_off=0, gain=None, extras=(),
                  extra_specs=(), out_shapes, out_specs, epilogue, name):
    m, k = x.shape
    assert m % bm == 0 and w.shape[0] == k and w.shape[1] % bn == 0
    has_norm = gain is not None
    in_specs = [pl.BlockSpec((bm, k), lambda i, j: (i, 0))]
    args = [x]
    if has_norm:
        in_specs.append(pl.BlockSpec((1, k), lambda i, j: (0, 0)))
        args.append(gain.reshape(1, k).astype(_F32))
    in_specs.append(pl.BlockSpec((k, bn), lambda i, j: (0, j + w_col_block_off)))
    args.append(w)
    in_specs.extend(extra_specs)
    args.extend(extras)
    scratch = [pltpu.VMEM((bm, k), _BF16)] if has_norm else []
    block_bytes = _nbytes((bm, k), x.dtype) + _nbytes((k, bn), w.dtype)
    for spec, arr in zip(extra_specs, extras):
        block_bytes += _nbytes([d for d in spec.block_shape if d is not None], arr.dtype)
    for spec, sh in zip(out_specs, out_shapes):
        block_bytes += _nbytes([d for d in spec.block_shape if d is not None], sh.dtype)
    scratch_bytes = _nbytes((bm, k), _BF16) if has_norm else 0
    body = functools.partial(_mm_body, has_norm=has_norm, n_extra=len(extras),
                             n_out=len(out_shapes), epilogue=epilogue)
    return pl.pallas_call(
        body,
        grid=(m // bm, n_col_blocks),
        in_specs=in_specs,
        out_specs=list(out_specs),
        out_shape=list(out_shapes),
        scratch_shapes=scratch,
        compiler_params=pltpu.CompilerParams(
            dimension_semantics=("parallel", "arbitrary"),
            vmem_limit_bytes=_vmem_limit(block_bytes, scratch_bytes, 3 * bm * bn * 4)),
        name=name,
    )(*args)


def _retention_qk(h, gain, w_in, cos, sin, seq):
    t, d = h.shape
    bm, bn = 1024, 1024
    dk = d // _RET_HEADS
    half = dk // 2
    n_q_blocks = d // bn
    k_scale = dk ** -0.5

    def epilogue(acc, i, j, ex, outs):
        o_ref = outs[0]
        cos_b = ex[0][...]
        sin_b = ex[1][...]
        scale = jnp.where(j < n_q_blocks, 1.0, k_scale).astype(_F32)
        for c in range(bn // dk):
            x1 = acc[:, c * dk:c * dk + half]
            x2 = acc[:, c * dk + half:(c + 1) * dk]
            o_ref[:, c * dk:c * dk + half] = ((x1 * cos_b - x2 * sin_b) * scale).astype(_BF16)
            o_ref[:, c * dk + half:(c + 1) * dk] = ((x1 * sin_b + x2 * cos_b) * scale).astype(_BF16)

    pos_blocks = seq // bm
    tab_spec = pl.BlockSpec((bm, half), lambda i, j: (i % pos_blocks, 0))
    return _fused_matmul(
        h, w_in, bm=bm, bn=bn, n_col_blocks=(2 * d) // bn, gain=gain,
        extras=(cos, sin), extra_specs=(tab_spec, tab_spec),
        out_shapes=[jax.ShapeDtypeStruct((t, 2 * d), _BF16)],
        out_specs=[pl.BlockSpec((bm, bn), lambda i, j: (i, j))],
        epilogue=epilogue, name="ret_qk")[0]


def _retention_vg(h, gain, w_in):
    t, d = h.shape
    bm, bn = 1024, 1024
    n_vg = w_in.shape[1] - 2 * d

    def epilogue(acc, i, j, ex, outs):
        outs[0][...] = acc.astype(_BF16)

    return _fused_matmul(
        h, w_in, bm=bm, bn=bn, n_col_blocks=n_vg // bn, w_col_block_off=(2 * d) // bn, gain=gain,
        out_shapes=[jax.ShapeDtypeStruct((t, n_vg), _BF16)],
        out_specs=[pl.BlockSpec((bm, bn), lambda i, j: (i, j))],
        epilogue=epilogue, name="ret_vg")[0]


def _matmul_residual(x, w, res, *, bm, bn, name):
    t = x.shape[0]
    n = w.shape[1]

    def epilogue(acc, i, j, ex, outs):
        outs[0][...] = ex[0][...] + acc

    spec = pl.BlockSpec((bm, bn), lambda i, j: (i, j))
    return _fused_matmul(
        x, w, bm=bm, bn=bn, n_col_blocks=n // bn, extras=(res,), extra_specs=(spec,),
        out_shapes=[jax.ShapeDtypeStruct((t, n), _F32)], out_specs=[spec],
        epilogue=epilogue, name=name)[0]


def _norm_matmul(h, gain, w, *, bm, bn, act, name):
    t = h.shape[0]
    n = w.shape[1]

    def epilogue(acc, i, j, ex, outs):
        outs[0][...] = act(acc).astype(_BF16)

    return _fused_matmul(
        h, w, bm=bm, bn=bn, n_col_blocks=n // bn, gain=gain,
        out_shapes=[jax.ShapeDtypeStruct((t, n), _BF16)],
        out_specs=[pl.BlockSpec((bm, bn), lambda i, j: (i, j))],
        epilogue=epilogue, name=name)[0]


def _identity(a):
    return a


def _rms_rows_inplace(o_ref, g_ref):
    g = g_ref[...]

    def chunk(c, carry):
        r = pl.multiple_of(c * _NORM_ROWS, _NORM_ROWS)
        x = o_ref[pl.ds(r, _NORM_ROWS), :]
        ms = jnp.mean(x * x, axis=-1, keepdims=True)
        o_ref[pl.ds(r, _NORM_ROWS), :] = (x * lax.rsqrt(ms + _RMS_EPS)) * g
        return carry

    lax.fori_loop(0, o_ref.shape[0] // _NORM_ROWS, chunk, 0, unroll=_NORM_UNROLL)


def _mlp_body(x_ref, g_ref, wu_ref, wd_ref, o_ref, xn_ref):
    f = pl.program_id(1)

    @pl.when(f == 0)
    def _():
        _rms_rows(x_ref, g_ref, xn_ref)
        o_ref[...] = x_ref[...]

    u = jnp.dot(xn_ref[...], wu_ref[...], preferred_element_type=_F32)
    r = jnp.maximum(u, 0.0)
    o_ref[...] += jnp.dot((r * r).astype(_BF16), wd_ref[...], preferred_element_type=_F32)


def _mlp(h, gain, w_up, w_down, name):
    t, d = h.shape
    ff = w_up.shape[1]
    bm, bf = 512, 1024
    block_bytes = 2 * _nbytes((bm, d), _F32) + _nbytes((d, bf), _BF16) + _nbytes((bf, d), _BF16)
    temp_bytes = _nbytes((bm, bf), _F32) * 2 + _nbytes((bm, d), _F32)
    return pl.pallas_call(
        _mlp_body,
        grid=(t // bm, ff // bf),
        in_specs=[
            pl.BlockSpec((bm, d), lambda i, f: (i, 0)),
            pl.BlockSpec((1, d), lambda i, f: (0, 0)),
            pl.BlockSpec((d, bf), lambda i, f: (0, f)),
            pl.BlockSpec((bf, d), lambda i, f: (f, 0)),
        ],
        out_specs=pl.BlockSpec((bm, d), lambda i, f: (i, 0)),
        out_shape=jax.ShapeDtypeStruct((t, d), _F32),
        scratch_shapes=[pltpu.VMEM((bm, d), _BF16)],
        compiler_params=pltpu.CompilerParams(
            dimension_semantics=("parallel", "arbitrary"),
            vmem_limit_bytes=_vmem_limit(block_bytes, _nbytes((bm, d), _BF16), temp_bytes)),
        name=name,
    )(h, gain.reshape(1, d).astype(_F32), w_up, w_down)


def _ple_body(*refs, final):
    if final:
        x_ref, g_ref, wg_ref, p_ref, wu_ref, gf_ref, o_ref, xn_ref = refs
    else:
        x_ref, g_ref, wg_ref, p_ref, wu_ref, o_ref, xn_ref = refs
    _rms_rows(x_ref, g_ref, xn_ref)
    gate = jnp.dot(xn_ref[...], wg_ref[...], preferred_element_type=_F32)
    up = jnp.dot(p_ref[...].astype(_BF16), wu_ref[...], preferred_element_type=_F32)
    o_ref[...] = x_ref[...] + up * jax.nn.sigmoid(gate)
    if final:
        _rms_rows_inplace(o_ref, gf_ref)


def _per_layer_embed(h, p_i, gain, w_up, w_gate, name, final_gain=None):
    t, d = h.shape
    pd = p_i.shape[1]
    bm = 512
    final = final_gain is not None
    row_spec = pl.BlockSpec((bm, d), lambda i: (i, 0))
    vec_spec = pl.BlockSpec((1, d), lambda i: (0, 0))
    in_specs = [row_spec, vec_spec, pl.BlockSpec((d, d), lambda i: (0, 0)),
                pl.BlockSpec((bm, pd), lambda i: (i, 0)), pl.BlockSpec((pd, d), lambda i: (0, 0))]
    args = [h, gain.reshape(1, d).astype(_F32), w_gate, p_i, w_up]
    if final:
        in_specs.append(vec_spec)
        args.append(final_gain.reshape(1, d).astype(_F32))
    block_bytes = (2 * _nbytes((bm, d), _F32) + _nbytes((d, d), _BF16) + _nbytes((bm, pd), _F32)
                   + _nbytes((pd, d), _BF16))
    return pl.pallas_call(
        functools.partial(_ple_body, final=final),
        grid=(t // bm,),
        in_specs=in_specs,
        out_specs=row_spec,
        out_shape=jax.ShapeDtypeStruct((t, d), _F32),
        scratch_shapes=[pltpu.VMEM((bm, d), _BF16)],
        compiler_params=pltpu.CompilerParams(
            dimension_semantics=("parallel",),
            vmem_limit_bytes=_vmem_limit(block_bytes, _nbytes((bm, d), _BF16),
                                         3 * _nbytes((bm, d), _F32))),
        name=name,
    )(*args)


def _shared_kv(h, gain, w_kv):
    t, d = h.shape
    n = w_kv.shape[1]
    bm, bn = 1024, 1024
    per = bm // _MOBA_BLOCK

    def epilogue(acc, i, j, ex, outs):
        outs[0][...] = acc.astype(_BF16)
        for r in range(per):
            blk = acc[r * _MOBA_BLOCK:(r + 1) * _MOBA_BLOCK, :]
            outs[1][0, r:r + 1, :] = jnp.mean(blk, axis=0, keepdims=True)

    kv, means = _fused_matmul(
        h, w_kv, bm=bm, bn=bn, n_col_blocks=n // bn, gain=gain,
        out_shapes=[jax.ShapeDtypeStruct((t, n), _BF16),
                    jax.ShapeDtypeStruct((t // bm, per, n), _F32)],
        out_specs=[pl.BlockSpec((bm, bn), lambda i, j: (i, j)),
                   pl.BlockSpec((1, per, bn), lambda i, j: (i, 0, j))],
        epilogue=epilogue, name="shared_kv")
    return kv, means.reshape(t // _MOBA_BLOCK, n)


def _ret_body(lg_ref, q_ref, k_ref, v_ref, g_ref, o_ref, state_ref, *, chunk, n_chunks):
    hd = pl.program_id(1)

    @pl.when(pl.program_id(2) == 0)
    def _():
        state_ref[...] = jnp.zeros_like(state_ref)

    lg = lg_ref[hd]
    ri = lax.broadcasted_iota(jnp.int32, (chunk, chunk), 0)
    ci = lax.broadcasted_iota(jnp.int32, (chunk, chunk), 1)
    diff = ri - ci
    decay = jnp.where(diff >= 0, jnp.exp(jnp.maximum(diff, 0).astype(_F32) * lg), 0.0)
    r1 = lax.broadcasted_iota(jnp.int32, (chunk, 1), 0)
    xi = jnp.exp((r1 + 1).astype(_F32) * lg)
    zeta = jnp.exp((chunk - 1 - r1).astype(_F32) * lg)
    chunk_decay = jnp.exp(jnp.full((1, 1), chunk, _F32) * lg)

    def step(c, carry):
        r = pl.multiple_of(c * chunk, chunk)
        qc = q_ref[pl.ds(r, chunk), :]
        kc = k_ref[pl.ds(r, chunk), :]
        vc = v_ref[pl.ds(r, chunk), :]
        scores = lax.dot_general(qc, kc, (((1,), (1,)), ((), ())),
                                 preferred_element_type=_F32) * decay
        intra = jnp.dot(scores.astype(_BF16), vc, preferred_element_type=_F32)
        state = state_ref[...]
        inter = jnp.dot(qc, state.astype(_BF16), preferred_element_type=_F32) * xi
        kz_t = jnp.transpose(kc.astype(_F32) * zeta).astype(_BF16)
        state_ref[...] = state * chunk_decay + jnp.dot(kz_t, vc, preferred_element_type=_F32)
        y = intra + inter
        mu = jnp.mean(y, axis=-1, keepdims=True)
        yc = y - mu
        var = jnp.mean(yc * yc, axis=-1, keepdims=True)
        yn = yc * lax.rsqrt(var + _GN_EPS)
        gate = g_ref[pl.ds(r, chunk), :].astype(_F32)
        o_ref[pl.ds(r, chunk), :] = (gate * jax.nn.sigmoid(gate) * yn).astype(_BF16)
        return carry

    lax.fori_loop(0, n_chunks, step, 0, unroll=True)


def _retention_core(qk, vg, batch, seq, d):
    t = qk.shape[0]
    dk = d // _RET_HEADS
    dv = 2 * dk
    vdim = _RET_HEADS * dv
    tb = 1024
    nt = seq // tb
    log_gamma = jnp.log1p(-jnp.exp2(-5.0 - jnp.arange(_RET_HEADS, dtype=_F32)))

    def row(b, h, s):
        return b * nt + s

    block_bytes = 2 * _nbytes((tb, dk), _BF16) + 3 * _nbytes((tb, dv), _BF16)
    return pl.pallas_call(
        functools.partial(_ret_body, chunk=_RET_CHUNK, n_chunks=tb // _RET_CHUNK),
        grid=(batch, _RET_HEADS, nt),
        in_specs=[
            pl.BlockSpec(memory_space=pltpu.SMEM),
            pl.BlockSpec((tb, dk), lambda b, h, s: (row(b, h, s), h)),
            pl.BlockSpec((tb, dk), lambda b, h, s: (row(b, h, s), _RET_HEADS + h)),
            pl.BlockSpec((tb, dv), lambda b, h, s: (row(b, h, s), h)),
            pl.BlockSpec((tb, dv), lambda b, h, s: (row(b, h, s), _RET_HEADS + h)),
        ],
        out_specs=pl.BlockSpec((tb, dv), lambda b, h, s: (row(b, h, s), h)),
        out_shape=jax.ShapeDtypeStruct((t, vdim), _BF16),
        scratch_shapes=[pltpu.VMEM((dk, dv), _F32)],
        compiler_params=pltpu.CompilerParams(
            dimension_semantics=("parallel", "parallel", "arbitrary"),
            vmem_limit_bytes=_vmem_limit(block_bytes, _nbytes((dk, dv), _F32), 8 << 20)),
        name="retention_core",
    )(log_gamma, qk, qk, vg, vg)


def _t5_thresholds():
    max_exact = _REL_BUCKETS // 2
    n_log = _REL_BUCKETS - max_exact
    ratio = _REL_MAX_DIST / max_exact
    thr = list(range(max_exact + 1))
    for kk in range(1, n_log):
        guess = int(math.floor(max_exact * ratio ** (kk / n_log)))
        n = max(guess - 2, max_exact)
        while not (n / max_exact) ** n_log >= ratio ** kk:
            n += 1
        thr.append(n)
    return thr


def _moba_body(rb_ref, q_ref, k_ref, v_ref, km_ref, o_ref, bias_ref, vt_ref, *,
               blk, nblk, n_tiles, thresholds, scale, topk, group):
    hd = pl.program_id(0)
    qb = pl.program_id(2)

    @pl.when((pl.program_id(1) == 0) & (qb == 0))
    def _():
        ki = lax.broadcasted_iota(jnp.int32, (blk, blk), 0)
        qi = lax.broadcasted_iota(jnp.int32, (blk, blk), 1)

        def build(dd, carry):
            dist = qi - ki + dd * blk
            val = jnp.full((blk, blk), rb_ref[0, hd], _F32)
            for b in range(1, _REL_BUCKETS):
                val = jnp.where(dist >= thresholds[b], rb_ref[b, hd], val)
            bias_ref[dd] = jnp.where(dist >= 0, val * _LOG2E, _NEG_INF)
            return carry

        lax.fori_loop(0, n_tiles, build, 0)
        bias_ref[n_tiles] = jnp.full((blk, blk), rb_ref[_REL_BUCKETS - 1, hd] * _LOG2E, _F32)

    @pl.when(qb == 0)
    def _():
        for n in range(nblk):
            v_blk = v_ref[n * blk:(n + 1) * blk, :].astype(_F32)
            vt_ref[:, n * blk:(n + 1) * blk] = jnp.transpose(v_blk).astype(_BF16)

    q = q_ref[...]
    km = km_ref[...].astype(_BF16)
    gate = lax.dot_general(km, q, (((1,), (1,)), ((), ())), preferred_element_type=_F32)
    row = lax.broadcasted_iota(jnp.int32, (nblk, blk), 0)
    past = row < qb
    g = jnp.where(past, gate, _NEG_INF)
    sel = jnp.zeros((nblk, blk), jnp.bool_)
    for _ in range(topk):
        mx = jnp.max(g, axis=0, keepdims=True)
        idx = jnp.min(jnp.where(g == mx, row, nblk), axis=0, keepdims=True)
        pick = row == idx
        sel = jnp.logical_or(sel, pick)
        g = jnp.where(pick, -jnp.inf, g)
    attend = jnp.logical_or(jnp.logical_and(sel, past), row == qb)
    pen = jnp.where(attend, 0.0, _NEG_INF).astype(_F32)
    sub = 8

    def attend_blocks(nb):
        s = []
        m_part = None
        for n in range(nb):
            tile = bias_ref[jnp.clip(qb - n, 0, n_tiles)]
            kq = lax.dot_general(k_ref[n * blk:(n + 1) * blk, :], q, (((1,), (1,)), ((), ())),
                                 preferred_element_type=_F32)
            sn = kq * (scale * _LOG2E) + tile
            s.append(sn)
            part = jnp.max(sn.reshape(blk // sub, sub, blk), axis=0) + pen[n:n + 1, :]
            m_part = part if m_part is None else jnp.maximum(m_part, part)
        m = jnp.max(m_part, axis=0, keepdims=True)
        l_part = None
        acc = None
        for n in range(nb):
            p = jnp.exp2(s[n] - (m - pen[n:n + 1, :]))
            pv = jnp.dot(vt_ref[:, n * blk:(n + 1) * blk], p.astype(_BF16),
                         preferred_element_type=_F32)
            part = jnp.sum(p.reshape(blk // sub, sub, blk), axis=0)
            l_part = part if l_part is None else l_part + part
            acc = pv if acc is None else acc + pv
        l = jnp.sum(l_part, axis=0, keepdims=True)
        o_ref[...] = jnp.transpose(acc / l).astype(_BF16)

    for gi in range(nblk // group):
        @pl.when(jnp.logical_and(qb >= gi * group, qb < (gi + 1) * group))
        def _(gi=gi):
            attend_blocks((gi + 1) * group)


_MOBA_GROUP = 4


def _moba_attention(q, kv, k_means, rel_bias, batch, seq):
    t, d = q.shape
    dh = d // _ATT_HEADS
    blk = _MOBA_BLOCK
    nblk = seq // blk
    thresholds = _t5_thresholds()
    n_tiles = 0
    while n_tiles * blk - (blk - 1) < thresholds[_REL_BUCKETS - 1]:
        n_tiles += 1
    n_tiles = min(n_tiles, nblk)
    group = math.gcd(_MOBA_GROUP, nblk)
    block_bytes = (2 * _nbytes((blk, dh), _BF16) + 2 * _nbytes((seq, dh), _BF16)
                   + _nbytes((nblk, dh), _F32))
    scratch_bytes = (n_tiles + 1) * blk * blk * 4 + _nbytes((dh, seq), _BF16)
    return pl.pallas_call(
        functools.partial(_moba_body, blk=blk, nblk=nblk, n_tiles=n_tiles, thresholds=thresholds,
                          scale=dh ** -0.5, topk=min(_MOBA_TOPK, nblk), group=group),
        grid=(_ATT_HEADS, batch, nblk),
        in_specs=[
            pl.BlockSpec(memory_space=pltpu.SMEM),
            pl.BlockSpec((blk, dh), lambda h, b, s: (b * nblk + s, h)),
            pl.BlockSpec((seq, dh), lambda h, b, s: (b, h)),
            pl.BlockSpec((seq, dh), lambda h, b, s: (b, _ATT_HEADS + h)),
            pl.BlockSpec((nblk, dh), lambda h, b, s: (b, h)),
        ],
        out_specs=pl.BlockSpec((blk, dh), lambda h, b, s: (b * nblk + s, h)),
        out_shape=jax.ShapeDtypeStruct((t, d), _BF16),
        scratch_shapes=[
            pltpu.VMEM((n_tiles + 1, blk, blk), _F32),
            pltpu.VMEM((dh, seq), _BF16),
        ],
        compiler_params=pltpu.CompilerParams(
            dimension_semantics=("arbitrary", "arbitrary", "arbitrary"),
            vmem_limit_bytes=_vmem_limit(block_bytes, scratch_bytes, 16 << 20)),
        name="moba_attention",
    )(rel_bias.astype(_F32), q, kv, kv, k_means)


def _rotary_tables(seq, half):
    inv = _ROPE_BASE ** (-jnp.arange(half, dtype=_F32) / half)
    ang = jnp.arange(seq).astype(_F32)[:, None] * inv[None, :]
    return jnp.cos(ang), jnp.sin(ang)


def kernel(x, p, ret_norm_g, ret_w_in, ret_w_out, kv_norm_g, w_kv, att_norm_g, att_w_q, att_w_o,
           rel_bias, mlp_norm_g, mlp_w_up, mlp_w_down, ple_norm_g, ple_w_up, ple_w_gate,
           final_norm_g):
    batch, seq, d = x.shape
    depth = p.shape[0]
    n_ret = ret_w_in.shape[0]
    t = batch * seq
    h = x.reshape(t, d)
    p2 = p.reshape(depth, t, p.shape[-1])
    cos, sin = _rotary_tables(seq, (d // _RET_HEADS) // 2)
    kv = k_means = None
    for i in range(depth):
        if i < n_ret:
            w_in = ret_w_in[i].astype(_BF16)
            qk = _retention_qk(h, ret_norm_g[i], w_in, cos, sin, seq)
            vg = _retention_vg(h, ret_norm_g[i], w_in)
            y = _retention_core(qk, vg, batch, seq, d)
            h = _matmul_residual(y, ret_w_out[i].astype(_BF16), h, bm=512, bn=1024, name="ret_out")
        else:
            j = i - n_ret
            q = _norm_matmul(h, att_norm_g[j], att_w_q[j].astype(_BF16), bm=1024, bn=1024,
                             act=_identity, name="att_q")
            o = _moba_attention(q, kv, k_means, rel_bias, batch, seq)
            h = _matmul_residual(o, att_w_o[j].astype(_BF16), h, bm=1024, bn=1024, name="att_out")
        h = _mlp(h, mlp_norm_g[i], mlp_w_up[i].astype(_BF16), mlp_w_down[i].astype(_BF16),
                 name=f"mlp{i}")
        h = _per_layer_embed(h, p2[i], ple_norm_g[i], ple_w_up[i].astype(_BF16),
                             ple_w_gate[i].astype(_BF16), name=f"ple{i}",
                             final_gain=final_norm_g if i == depth - 1 else None)
        if i == n_ret - 1:
            kv, k_means = _shared_kv(h, kv_norm_g, w_kv.astype(_BF16))
    return h.reshape(batch, seq, d)
```

```python
import functools
import math

import jax
import jax.numpy as jnp
from jax import lax
from jax.experimental import pallas as pl
from jax.experimental.pallas import tpu as pltpu

_F32 = jnp.float32
_BF16 = jnp.bfloat16

_RET_HEADS = 8
_RET_CHUNK = 128
_ROPE_BASE = 10000.0
_GN_EPS = 1e-5
_ATT_HEADS = 16
_MOBA_BLOCK = 256
_MOBA_TOPK = 3
_REL_BUCKETS = 32
_REL_MAX_DIST = 1024
_RMS_EPS = 1e-6
_NEG_INF = -1e30
_LOG2E = math.log2(math.e)

_V7X_VMEM_BUDGET = 60 * 1024 * 1024
_NORM_ROWS = 32
_NORM_UNROLL = 4


def _vmem_limit(block_bytes, scratch_bytes=0, temp_bytes=0):
    need = 2 * block_bytes + scratch_bytes + temp_bytes + (4 << 20)
    return int(min(max(need, 16 << 20), _V7X_VMEM_BUDGET))


def _nbytes(shape, dtype):
    return math.prod(shape) * jnp.dtype(dtype).itemsize


def _rms_rows(x_ref, g_ref, xn_ref):
    g = g_ref[...]

    def chunk(c, carry):
        r = pl.multiple_of(c * _NORM_ROWS, _NORM_ROWS)
        x = x_ref[pl.ds(r, _NORM_ROWS), :].astype(_F32)
        ms = jnp.mean(x * x, axis=-1, keepdims=True)
        xn_ref[pl.ds(r, _NORM_ROWS), :] = ((x * lax.rsqrt(ms + _RMS_EPS)) * g).astype(_BF16)
        return carry

    lax.fori_loop(0, x_ref.shape[0] // _NORM_ROWS, chunk, 0, unroll=_NORM_UNROLL)


def _mm_body(*refs, has_norm, n_extra, n_out, epilogue):
    x_ref = refs[0]
    pos = 1
    g_ref = None
    if has_norm:
        g_ref = refs[pos]
        pos += 1
    w_ref = refs[pos]
    pos += 1
    extra_refs = refs[pos:pos + n_extra]
    pos += n_extra
    out_refs = refs[pos:pos + n_out]
    pos += n_out
    i = pl.program_id(0)
    j = pl.program_id(1)
    if has_norm:
        xn_ref = refs[pos]

        @pl.when(j == 0)
        def _():
            _rms_rows(x_ref, g_ref, xn_ref)

        lhs = xn_ref[...]
    else:
        lhs = x_ref[...]
    acc = jnp.dot(lhs, w_ref[...], preferred_element_type=_F32)
    epilogue(acc, i, j, extra_refs, out_refs)


def _fused_matmul(x, w, *, bm, bn, n_col_blocks, w_col_block_off=0, gain=None, extras=(),
                  extra_specs=(), out_shapes, out_specs, epilogue, name):
    m, k = x.shape
    assert m % bm == 0 and w.shape[0] == k and w.shape[1] % bn == 0
    has_norm = gain is not None
    in_specs = [pl.BlockSpec((bm, k), lambda i, j: (i, 0))]
    args = [x]
    if has_norm:
        in_specs.append(pl.BlockSpec((1, k), lambda i, j: (0, 0)))
        args.append(gain.reshape(1, k).astype(_F32))
    in_specs.append(pl.BlockSpec((k, bn), lambda i, j: (0, j + w_col_block_off)))
    args.append(w)
    in_specs.extend(extra_specs)
    args.extend(extras)
    scratch = [pltpu.VMEM((bm, k), _BF16)] if has_norm else []
    block_bytes = _nbytes((bm, k), x.dtype) + _nbytes((k, bn), w.dtype)
    for spec, arr in zip(extra_specs, extras):
        block_bytes += _nbytes([d for d in spec.block_shape if d is not None], arr.dtype)
    for spec, sh in zip(out_specs, out_shapes):
        block_bytes += _nbytes([d for d in spec.block_shape if d is not None], sh.dtype)
    scratch_bytes = _nbytes((bm, k), _BF16) if has_norm else 0
    body = functools.partial(_mm_body, has_norm=has_norm, n_extra=len(extras),
                             n_out=len(out_shapes), epilogue=epilogue)
    return pl.pallas_call(
        body,
        grid=(m // bm, n_col_blocks),
        in_specs=in_specs,
        out_specs=list(out_specs),
        out_shape=list(out_shapes),
        scratch_shapes=scratch,
        compiler_params=pltpu.CompilerParams(
            dimension_semantics=("parallel", "arbitrary"),
            vmem_limit_bytes=_vmem_limit(block_bytes, scratch_bytes, 3 * bm * bn * 4)),
        name=name,
    )(*args)


def _retention_qk(h, gain, w_in, cos, sin, seq):
    t, d = h.shape
    bm, bn = 1024, 1024
    dk = d // _RET_HEADS
    half = dk // 2
    n_q_blocks = d // bn
    k_scale = dk ** -0.5

    def epilogue(acc, i, j, ex, outs):
        o_ref = outs[0]
        cos_b = ex[0][...]
        sin_b = ex[1][...]
        scale = jnp.where(j < n_q_blocks, 1.0, k_scale).astype(_F32)
        for c in range(bn // dk):
            x1 = acc[:, c * dk:c * dk + half]
            x2 = acc[:, c * dk + half:(c + 1) * dk]
            o_ref[:, c * dk:c * dk + half] = ((x1 * cos_b - x2 * sin_b) * scale).astype(_BF16)
            o_ref[:, c * dk + half:(c + 1) * dk] = ((x1 * sin_b + x2 * cos_b) * scale).astype(_BF16)

    pos_blocks = seq // bm
    tab_spec = pl.BlockSpec((bm, half), lambda i, j: (i % pos_blocks, 0))
    return _fused_matmul(
        h, w_in, bm=bm, bn=bn, n_col_blocks=(2 * d) // bn, gain=gain,
        extras=(cos, sin), extra_specs=(tab_spec, tab_spec),
        out_shapes=[jax.ShapeDtypeStruct((t, 2 * d), _BF16)],
        out_specs=[pl.BlockSpec((bm, bn), lambda i, j: (i, j))],
        epilogue=epilogue, name="ret_qk")[0]


def _retention_vg(h, gain, w_in):
    t, d = h.shape
    bm, bn = 1024, 1024
    n_vg = w_in.shape[1] - 2 * d

    def epilogue(acc, i, j, ex, outs):
        outs[0][...] = acc.astype(_BF16)

    return _fused_matmul(
        h, w_in, bm=bm, bn=bn, n_col_blocks=n_vg // bn, w_col_block_off=(2 * d) // bn, gain=gain,
        out_shapes=[jax.ShapeDtypeStruct((t, n_vg), _BF16)],
        out_specs=[pl.BlockSpec((bm, bn), lambda i, j: (i, j))],
        epilogue=epilogue, name="ret_vg")[0]


def _matmul_residual(x, w, res, *, bm, bn, name):
    t = x.shape[0]
    n = w.shape[1]

    def epilogue(acc, i, j, ex, outs):
        outs[0][...] = ex[0][...] + acc

    spec = pl.BlockSpec((bm, bn), lambda i, j: (i, j))
    return _fused_matmul(
        x, w, bm=bm, bn=bn, n_col_blocks=n // bn, extras=(res,), extra_specs=(spec,),
        out_shapes=[jax.ShapeDtypeStruct((t, n), _F32)], out_specs=[spec],
        epilogue=epilogue, name=name)[0]


def _norm_matmul(h, gain, w, *, bm, bn, act, name):
    t = h.shape[0]
    n = w.shape[1]

    def epilogue(acc, i, j, ex, outs):
        outs[0][...] = act(acc).astype(_BF16)

    return _fused_matmul(
        h, w, bm=bm, bn=bn, n_col_blocks=n // bn, gain=gain,
        out_shapes=[jax.ShapeDtypeStruct((t, n), _BF16)],
        out_specs=[pl.BlockSpec((bm, bn), lambda i, j: (i, j))],
        epilogue=epilogue, name=name)[0]


def _identity(a):
    return a


def _rms_rows_inplace(o_ref, g_ref):
    g = g_ref[...]

    def chunk(c, carry):
        r = pl.multiple_of(c * _NORM_ROWS, _NORM_ROWS)
        x = o_ref[pl.ds(r, _NORM_ROWS), :]
        ms = jnp.mean(x * x, axis=-1, keepdims=True)
        o_ref[pl.ds(r, _NORM_ROWS), :] = (x * lax.rsqrt(ms + _RMS_EPS)) * g
        return carry

    lax.fori_loop(0, o_ref.shape[0] // _NORM_ROWS, chunk, 0, unroll=_NORM_UNROLL)


def _mlp_body(x_ref, g_ref, wu_ref, wd_ref, o_ref, xn_ref):
    f = pl.program_id(1)

    @pl.when(f == 0)
    def _():
        _rms_rows(x_ref, g_ref, xn_ref)
        o_ref[...] = x_ref[...]

    u = jnp.dot(xn_ref[...], wu_ref[...], preferred_element_type=_F32)
    r = jnp.maximum(u, 0.0)
    o_ref[...] += jnp.dot((r * r).astype(_BF16), wd_ref[...], preferred_element_type=_F32)


def _mlp(h, gain, w_up, w_down, name):
    t, d = h.shape
    ff = w_up.shape[1]
    bm, bf = 512, 1024
    block_bytes = 2 * _nbytes((bm, d), _F32) + _nbytes((d, bf), _BF16) + _nbytes((bf, d), _BF16)
    temp_bytes = _nbytes((bm, bf), _F32) * 2 + _nbytes((bm, d), _F32)
    return pl.pallas_call(
        _mlp_body,
        grid=(t // bm, ff // bf),
        in_specs=[
            pl.BlockSpec((bm, d), lambda i, f: (i, 0)),
            pl.BlockSpec((1, d), lambda i, f: (0, 0)),
            pl.BlockSpec((d, bf), lambda i, f: (0, f)),
            pl.BlockSpec((bf, d), lambda i, f: (f, 0)),
        ],
        out_specs=pl.BlockSpec((bm, d), lambda i, f: (i, 0)),
        out_shape=jax.ShapeDtypeStruct((t, d), _F32),
        scratch_shapes=[pltpu.VMEM((bm, d), _BF16)],
        compiler_params=pltpu.CompilerParams(
            dimension_semantics=("parallel", "arbitrary"),
            vmem_limit_bytes=_vmem_limit(block_bytes, _nbytes((bm, d), _BF16), temp_bytes)),
        name=name,
    )(h, gain.reshape(1, d).astype(_F32), w_up, w_down)


def _ple_body(*refs, final):
    if final:
        x_ref, g_ref, wg_ref, p_ref, wu_ref, gf_ref, o_ref, xn_ref = refs
    else:
        x_ref, g_ref, wg_ref, p_ref, wu_ref, o_ref, xn_ref = refs
    _rms_rows(x_ref, g_ref, xn_ref)
    gate = jnp.dot(xn_ref[...], wg_ref[...], preferred_element_type=_F32)
    up = jnp.dot(p_ref[...].astype(_BF16), wu_ref[...], preferred_element_type=_F32)
    o_ref[...] = x_ref[...] + up * jax.nn.sigmoid(gate)
    if final:
        _rms_rows_inplace(o_ref, gf_ref)


def _per_layer_embed(h, p_i, gain, w_up, w_gate, name, final_gain=None):
    t, d = h.shape
    pd = p_i.shape[1]
    bm = 512
    final = final_gain is not None
    row_spec = pl.BlockSpec((bm, d), lambda i: (i, 0))
    vec_spec = pl.BlockSpec((1, d), lambda i: (0, 0))
    in_specs = [row_spec, vec_spec, pl.BlockSpec((d, d), lambda i: (0, 0)),
                pl.BlockSpec((bm, pd), lambda i: (i, 0)), pl.BlockSpec((pd, d), lambda i: (0, 0))]
    args = [h, gain.reshape(1, d).astype(_F32), w_gate, p_i, w_up]
    if final:
        in_specs.append(vec_spec)
        args.append(final_gain.reshape(1, d).astype(_F32))
    block_bytes = (2 * _nbytes((bm, d), _F32) + _nbytes((d, d), _BF16) + _nbytes((bm, pd), _F32)
                   + _nbytes((pd, d), _BF16))
    return pl.pallas_call(
        functools.partial(_ple_body, final=final),
        grid=(t // bm,),
        in_specs=in_specs,
        out_specs=row_spec,
        out_shape=jax.ShapeDtypeStruct((t, d), _F32),
        scratch_shapes=[pltpu.VMEM((bm, d), _BF16)],
        compiler_params=pltpu.CompilerParams(
            dimension_semantics=("parallel",),
            vmem_limit_bytes=_vmem_limit(block_bytes, _nbytes((bm, d), _BF16),
                                         3 * _nbytes((bm, d), _F32))),
        name=name,
    )(*args)


def _shared_kv(h, gain, w_kv):
    t, d = h.shape
    n = w_kv.shape[1]
    bm, bn = 1024, 1024
    per = bm // _MOBA_BLOCK

    def epilogue(acc, i, j, ex, outs):
        outs[0][...] = acc.astype(_BF16)
        for r in range(per):
            blk = acc[r * _MOBA_BLOCK:(r + 1) * _MOBA_BLOCK, :]
            outs[1][0, r:r + 1, :] = jnp.mean(blk, axis=0, keepdims=True)

    kv, means = _fused_matmul(
        h, w_kv, bm=bm, bn=bn, n_col_blocks=n // bn, gain=gain,
        out_shapes=[jax.ShapeDtypeStruct((t, n), _BF16),
                    jax.ShapeDtypeStruct((t // bm, per, n), _F32)],
        out_specs=[pl.BlockSpec((bm, bn), lambda i, j: (i, j)),
                   pl.BlockSpec((1, per, bn), lambda i, j: (i, 0, j))],
        epilogue=epilogue, name="shared_kv")
    return kv, means.reshape(t // _MOBA_BLOCK, n)


def _ret_body(lg_ref, q_ref, k_ref, v_ref, g_ref, o_ref, state_ref, *, chunk, n_chunks):
    hd = pl.program_id(1)

    @pl.when(pl.program_id(2) == 0)
    def _():
        state_ref[...] = jnp.zeros_like(state_ref)

    lg = lg_ref[hd]
    ri = lax.broadcasted_iota(jnp.int32, (chunk, chunk), 0)
    ci = lax.broadcasted_iota(jnp.int32, (chunk, chunk), 1)
    diff = ri - ci
    decay = jnp.where(diff >= 0, jnp.exp(jnp.maximum(diff, 0).astype(_F32) * lg), 0.0)
    r1 = lax.broadcasted_iota(jnp.int32, (chunk, 1), 0)
    xi = jnp.exp((r1 + 1).astype(_F32) * lg)
    zeta = jnp.exp((chunk - 1 - r1).astype(_F32) * lg)
    chunk_decay = jnp.exp(jnp.full((1, 1), chunk, _F32) * lg)

    def step(c, carry):
        r = pl.multiple_of(c * chunk, chunk)
        qc = q_ref[pl.ds(r, chunk), :]
        kc = k_ref[pl.ds(r, chunk), :]
        vc = v_ref[pl.ds(r, chunk), :]
        scores = lax.dot_general(qc, kc, (((1,), (1,)), ((), ())),
                                 preferred_element_type=_F32) * decay
        intra = jnp.dot(scores.astype(_BF16), vc, preferred_element_type=_F32)
        state = state_ref[...]
        inter = jnp.dot(qc, state.astype(_BF16), preferred_element_type=_F32) * xi
        kz_t = jnp.transpose(kc.astype(_F32) * zeta).astype(_BF16)
        state_ref[...] = state * chunk_decay + jnp.dot(kz_t, vc, preferred_element_type=_F32)
        y = intra + inter
        mu = jnp.mean(y, axis=-1, keepdims=True)
        yc = y - mu
        var = jnp.mean(yc * yc, axis=-1, keepdims=True)
        yn = yc * lax.rsqrt(var + _GN_EPS)
        gate = g_ref[pl.ds(r, chunk), :].astype(_F32)
        o_ref[pl.ds(r, chunk), :] = (gate * jax.nn.sigmoid(gate) * yn).astype(_BF16)
        return carry

    lax.fori_loop(0, n_chunks, step, 0, unroll=True)


def _retention_core(qk, vg, batch, seq, d):
    t = qk.shape[0]
    dk = d // _RET_HEADS
    dv = 2 * dk
    vdim = _RET_HEADS * dv
    tb = 1024
    nt = seq // tb
    log_gamma = jnp.log1p(-jnp.exp2(-5.0 - jnp.arange(_RET_HEADS, dtype=_F32)))

    def row(b, h, s):
        return b * nt + s

    block_bytes = 2 * _nbytes((tb, dk), _BF16) + 3 * _nbytes((tb, dv), _BF16)
    return pl.pallas_call(
        functools.partial(_ret_body, chunk=_RET_CHUNK, n_chunks=tb // _RET_CHUNK),
        grid=(batch, _RET_HEADS, nt),
        in_specs=[
            pl.BlockSpec(memory_space=pltpu.SMEM),
            pl.BlockSpec((tb, dk), lambda b, h, s: (row(b, h, s), h)),
            pl.BlockSpec((tb, dk), lambda b, h, s: (row(b, h, s), _RET_HEADS + h)),
            pl.BlockSpec((tb, dv), lambda b, h, s: (row(b, h, s), h)),
            pl.BlockSpec((tb, dv), lambda b, h, s: (row(b, h, s), _RET_HEADS + h)),
        ],
        out_specs=pl.BlockSpec((tb, dv), lambda b, h, s: (row(b, h, s), h)),
        out_shape=jax.ShapeDtypeStruct((t, vdim), _BF16),
        scratch_shapes=[pltpu.VMEM((dk, dv), _F32)],
        compiler_params=pltpu.CompilerParams(
            dimension_semantics=("parallel", "parallel", "arbitrary"),
            vmem_limit_bytes=_vmem_limit(block_bytes, _nbytes((dk, dv), _F32), 8 << 20)),
        name="retention_core",
    )(log_gamma, qk, qk, vg, vg)


def _t5_thresholds():
    max_exact = _REL_BUCKETS // 2
    n_log = _REL_BUCKETS - max_exact
    ratio = _REL_MAX_DIST / max_exact
    thr = list(range(max_exact + 1))
    for kk in range(1, n_log):
        guess = int(math.floor(max_exact * ratio ** (kk / n_log)))
        n = max(guess - 2, max_exact)
        while not (n / max_exact) ** n_log >= ratio ** kk:
            n += 1
        thr.append(n)
    return thr


def _moba_body(rb_ref, q_ref, k_ref, v_ref, km_ref, o_ref, bias_ref, vt_ref, *,
               blk, nblk, n_tiles, thresholds, scale, topk, group):
    hd = pl.program_id(0)
    gstep = pl.program_id(2)

    @pl.when((pl.program_id(1) == 0) & (gstep == 0))
    def _():
        ki = lax.broadcasted_iota(jnp.int32, (blk, blk), 0)
        qi = lax.broadcasted_iota(jnp.int32, (blk, blk), 1)

        def build(dd, carry):
            dist = qi - ki + dd * blk
            val = jnp.full((blk, blk), rb_ref[0, hd], _F32)
            for b in range(1, _REL_BUCKETS):
                val = jnp.where(dist >= thresholds[b], rb_ref[b, hd], val)
            bias_ref[dd] = jnp.where(dist >= 0, val * _LOG2E, _NEG_INF)
            return carry

        lax.fori_loop(0, n_tiles, build, 0)
        bias_ref[n_tiles] = jnp.full((blk, blk), rb_ref[_REL_BUCKETS - 1, hd] * _LOG2E, _F32)

    @pl.when(gstep == 0)
    def _():
        for n in range(nblk):
            v_blk = v_ref[n * blk:(n + 1) * blk, :].astype(_F32)
            vt_ref[:, n * blk:(n + 1) * blk] = jnp.transpose(v_blk).astype(_BF16)

    km = km_ref[...].astype(_BF16)
    row = lax.broadcasted_iota(jnp.int32, (nblk, blk), 0)
    sub = 8

    def attend_query_block(qb, r):
        q = q_ref[r * blk:(r + 1) * blk, :]
        gate = lax.dot_general(km, q, (((1,), (1,)), ((), ())), preferred_element_type=_F32)
        past = row < qb
        g = jnp.where(past, gate, _NEG_INF)
        sel = jnp.zeros((nblk, blk), jnp.bool_)
        for _ in range(topk):
            mx = jnp.max(g, axis=0, keepdims=True)
            idx = jnp.min(jnp.where(g == mx, row, nblk), axis=0, keepdims=True)
            pick = row == idx
            sel = jnp.logical_or(sel, pick)
            g = jnp.where(pick, -jnp.inf, g)
        attend = jnp.logical_or(jnp.logical_and(sel, past), row == qb)
        pen = jnp.where(attend, 0.0, _NEG_INF).astype(_F32)
        s = []
        m_part = None
        for n in range(qb + 1):
            kq = lax.dot_general(k_ref[n * blk:(n + 1) * blk, :], q, (((1,), (1,)), ((), ())),
                                 preferred_element_type=_F32)
            sn = kq * (scale * _LOG2E) + bias_ref[min(qb - n, n_tiles)]
            s.append(sn)
            part = jnp.max(sn.reshape(blk // sub, sub, blk), axis=0) + pen[n:n + 1, :]
            m_part = part if m_part is None else jnp.maximum(m_part, part)
        m = jnp.max(m_part, axis=0, keepdims=True)
        l_part = None
        acc = None
        for n in range(qb + 1):
            p = jnp.exp2(s[n] - (m - pen[n:n + 1, :]))
            pv = jnp.dot(vt_ref[:, n * blk:(n + 1) * blk], p.astype(_BF16),
                         preferred_element_type=_F32)
            part = jnp.sum(p.reshape(blk // sub, sub, blk), axis=0)
            l_part = part if l_part is None else l_part + part
            acc = pv if acc is None else acc + pv
        l = jnp.sum(l_part, axis=0, keepdims=True)
        o_ref[r * blk:(r + 1) * blk, :] = jnp.transpose(acc / l).astype(_BF16)

    for gi in range(nblk // group):
        @pl.when(gstep == gi)
        def _(gi=gi):
            for r in range(group):
                attend_query_block(gi * group + r, r)


_MOBA_GROUP = 4


def _moba_attention(q, kv, k_means, rel_bias, batch, seq):
    t, d = q.shape
    dh = d // _ATT_HEADS
    blk = _MOBA_BLOCK
    nblk = seq // blk
    thresholds = _t5_thresholds()
    n_tiles = 0
    while n_tiles * blk - (blk - 1) < thresholds[_REL_BUCKETS - 1]:
        n_tiles += 1
    n_tiles = min(n_tiles, nblk)
    group = math.gcd(_MOBA_GROUP, nblk)
    steps = nblk // group
    block_bytes = (2 * _nbytes((group * blk, dh), _BF16) + 2 * _nbytes((seq, dh), _BF16)
                   + _nbytes((nblk, dh), _F32))
    scratch_bytes = (n_tiles + 1) * blk * blk * 4 + _nbytes((dh, seq), _BF16)
    return pl.pallas_call(
        functools.partial(_moba_body, blk=blk, nblk=nblk, n_tiles=n_tiles, thresholds=thresholds,
                          scale=dh ** -0.5, topk=min(_MOBA_TOPK, nblk), group=group),
        grid=(_ATT_HEADS, batch, steps),
        in_specs=[
            pl.BlockSpec(memory_space=pltpu.SMEM),
            pl.BlockSpec((group * blk, dh), lambda h, b, s: (b * steps + s, h)),
            pl.BlockSpec((seq, dh), lambda h, b, s: (b, h)),
            pl.BlockSpec((seq, dh), lambda h, b, s: (b, _ATT_HEADS + h)),
            pl.BlockSpec((nblk, dh), lambda h, b, s: (b, h)),
        ],
        out_specs=pl.BlockSpec((group * blk, dh), lambda h, b, s: (b * steps + s, h)),
        out_shape=jax.ShapeDtypeStruct((t, d), _BF16),
        scratch_shapes=[
            pltpu.VMEM((n_tiles + 1, blk, blk), _F32),
            pltpu.VMEM((dh, seq), _BF16),
        ],
        compiler_params=pltpu.CompilerParams(
            dimension_semantics=("arbitrary", "arbitrary", "arbitrary"),
            vmem_limit_bytes=_vmem_limit(block_bytes, scratch_bytes, 16 << 20)),
        name="moba_attention",
    )(rel_bias.astype(_F32), q, kv, kv, k_means)


def _rotary_tables(seq, half):
    inv = _ROPE_BASE ** (-jnp.arange(half, dtype=_F32) / half)
    ang = jnp.arange(seq).astype(_F32)[:, None] * inv[None, :]
    return jnp.cos(ang), jnp.sin(ang)


def kernel(x, p, ret_norm_g, ret_w_in, ret_w_out, kv_norm_g, w_kv, att_norm_g, att_w_q, att_w_o,
           rel_bias, mlp_norm_g, mlp_w_up, mlp_w_down, ple_norm_g, ple_w_up, ple_w_gate,
           final_norm_g):
    batch, seq, d = x.shape
    depth = p.shape[0]
    n_ret = ret_w_in.shape[0]
    t = batch * seq
    h = x.reshape(t, d)
    p2 = p.reshape(depth, t, p.shape[-1])
    cos, sin = _rotary_tables(seq, (d // _RET_HEADS) // 2)
    kv = k_means = None
    for i in range(depth):
        if i < n_ret:
            w_in = ret_w_in[i].astype(_BF16)
            qk = _retention_qk(h, ret_norm_g[i], w_in, cos, sin, seq)
            vg = _retention_vg(h, ret_norm_g[i], w_in)
            y = _retention_core(qk, vg, batch, seq, d)
            h = _matmul_residual(y, ret_w_out[i].astype(_BF16), h, bm=512, bn=1024, name="ret_out")
        else:
            j = i - n_ret
            q = _norm_matmul(h, att_norm_g[j], att_w_q[j].astype(_BF16), bm=1024, bn=1024,
                             act=_identity, name="att_q")
            o = _moba_attention(q, kv, k_means, rel_bias, batch, seq)
            h = _matmul_residual(o, att_w_o[j].astype(_BF16), h, bm=1024, bn=1024, name="att_out")
        h = _mlp(h, mlp_norm_g[i], mlp_w_up[i].astype(_BF16), mlp_w_down[i].astype(_BF16),
                 name=f"mlp{i}")
        h = _per_layer_embed(h, p2[i], ple_norm_g[i], ple_w_up[i].astype(_BF16),
                             ple_w_gate[i].astype(_BF16), name=f"ple{i}",
                             final_gain=final_norm_g if i == depth - 1 else None)
        if i == n_ret - 1:
            kv, k_means = _shared_kv(h, kv_norm_g, w_kv.astype(_BF16))
    return h.reshape(batch, seq, d)
```

```python
import functools
import math

import jax
import jax.numpy as jnp
from jax import lax
from jax.experimental import pallas as pl
from jax.experimental.pallas import tpu as pltpu

_F32 = jnp.float32
_BF16 = jnp.bfloat16

_RET_HEADS = 8
_RET_CHUNK = 128
_ROPE_BASE = 10000.0
_GN_EPS = 1e-5
_ATT_HEADS = 16
_MOBA_BLOCK = 256
_MOBA_TOPK = 3
_REL_BUCKETS = 32
_REL_MAX_DIST = 1024
_RMS_EPS = 1e-6
_NEG_INF = -1e30
_LOG2E = math.log2(math.e)

_V7X_VMEM_BUDGET = 60 * 1024 * 1024
_NORM_ROWS = 32
_NORM_UNROLL = 4


def _vmem_limit(block_bytes, scratch_bytes=0, temp_bytes=0):
    need = 2 * block_bytes + scratch_bytes + temp_bytes + (4 << 20)
    return int(min(max(need, 16 << 20), _V7X_VMEM_BUDGET))


def _nbytes(shape, dtype):
    return math.prod(shape) * jnp.dtype(dtype).itemsize


def _rms_rows(x_ref, g_ref, xn_ref):
    g = g_ref[...]

    def chunk(c, carry):
        r = pl.multiple_of(c * _NORM_ROWS, _NORM_ROWS)
        x = x_ref[pl.ds(r, _NORM_ROWS), :].astype(_F32)
        ms = jnp.mean(x * x, axis=-1, keepdims=True)
        xn_ref[pl.ds(r, _NORM_ROWS), :] = ((x * lax.rsqrt(ms + _RMS_EPS)) * g).astype(_BF16)
        return carry

    lax.fori_loop(0, x_ref.shape[0] // _NORM_ROWS, chunk, 0, unroll=_NORM_UNROLL)


def _mm_body(*refs, has_norm, n_extra, n_out, epilogue):
    x_ref = refs[0]
    pos = 1
    g_ref = None
    if has_norm:
        g_ref = refs[pos]
        pos += 1
    w_ref = refs[pos]
    pos += 1
    extra_refs = refs[pos:pos + n_extra]
    pos += n_extra
    out_refs = refs[pos:pos + n_out]
    pos += n_out
    i = pl.program_id(0)
    j = pl.program_id(1)
    if has_norm:
        xn_ref = refs[pos]

        @pl.when(j == 0)
        def _():
            _rms_rows(x_ref, g_ref, xn_ref)

        lhs = xn_ref[...]
    else:
        lhs = x_ref[...]
    acc = jnp.dot(lhs, w_ref[...], preferred_element_type=_F32)
    epilogue(acc, i, j, extra_refs, out_refs)


def _fused_matmul(x, w, *, bm, bn, n_col_blocks, w_col_block_off=0, gain=None, extras=(),
                  extra_specs=(), out_shapes, out_specs, epilogue, name):
    m, k = x.shape
    assert m % bm == 0 and w.shape[0] == k and w.shape[1] % bn == 0
    has_norm = gain is not None
    in_specs = [pl.BlockSpec((bm, k), lambda i, j: (i, 0))]
    args = [x]
    if has_norm:
        in_specs.append(pl.BlockSpec((1, k), lambda i, j: (0, 0)))
        args.append(gain.reshape(1, k).astype(_F32))
    in_specs.append(pl.BlockSpec((k, bn), lambda i, j: (0, j + w_col_block_off)))
    args.append(w)
    in_specs.extend(extra_specs)
    args.extend(extras)
    scratch = [pltpu.VMEM((bm, k), _BF16)] if has_norm else []
    block_bytes = _nbytes((bm, k), x.dtype) + _nbytes((k, bn), w.dtype)
    for spec, arr in zip(extra_specs, extras):
        block_bytes += _nbytes([d for d in spec.block_shape if d is not None], arr.dtype)
    for spec, sh in zip(out_specs, out_shapes):
        block_bytes += _nbytes([d for d in spec.block_shape if d is not None], sh.dtype)
    scratch_bytes = _nbytes((bm, k), _BF16) if has_norm else 0
    body = functools.partial(_mm_body, has_norm=has_norm, n_extra=len(extras),
                             n_out=len(out_shapes), epilogue=epilogue)
    return pl.pallas_call(
        body,
        grid=(m // bm, n_col_blocks),
        in_specs=in_specs,
        out_specs=list(out_specs),
        out_shape=list(out_shapes),
        scratch_shapes=scratch,
        compiler_params=pltpu.CompilerParams(
            dimension_semantics=("parallel", "arbitrary"),
            vmem_limit_bytes=_vmem_limit(block_bytes, scratch_bytes, 3 * bm * bn * 4)),
        name=name,
    )(*args)


def _retention_qk(h, gain, w_in, cos, sin, seq):
    t, d = h.shape
    bm, bn = 1024, 1024
    dk = d // _RET_HEADS
    half = dk // 2
    n_q_blocks = d // bn
    k_scale = dk ** -0.5

    def epilogue(acc, i, j, ex, outs):
        o_ref = outs[0]
        cos_b = ex[0][...]
        sin_b = ex[1][...]
        scale = jnp.where(j < n_q_blocks, 1.0, k_scale).astype(_F32)
        for c in range(bn // dk):
            x1 = acc[:, c * dk:c * dk + half]
            x2 = acc[:, c * dk + half:(c + 1) * dk]
            o_ref[:, c * dk:c * dk + half] = ((x1 * cos_b - x2 * sin_b) * scale).astype(_BF16)
            o_ref[:, c * dk + half:(c + 1) * dk] = ((x1 * sin_b + x2 * cos_b) * scale).astype(_BF16)

    pos_blocks = seq // bm
    tab_spec = pl.BlockSpec((bm, half), lambda i, j: (i % pos_blocks, 0))
    return _fused_matmul(
        h, w_in, bm=bm, bn=bn, n_col_blocks=(2 * d) // bn, gain=gain,
        extras=(cos, sin), extra_specs=(tab_spec, tab_spec),
        out_shapes=[jax.ShapeDtypeStruct((t, 2 * d), _BF16)],
        out_specs=[pl.BlockSpec((bm, bn), lambda i, j: (i, j))],
        epilogue=epilogue, name="ret_qk")[0]


def _retention_vg(h, gain, w_in):
    t, d = h.shape
    bm, bn = 1024, 1024
    n_vg = w_in.shape[1] - 2 * d

    def epilogue(acc, i, j, ex, outs):
        outs[0][...] = acc.astype(_BF16)

    return _fused_matmul(
        h, w_in, bm=bm, bn=bn, n_col_blocks=n_vg // bn, w_col_block_off=(2 * d) // bn, gain=gain,
        out_shapes=[jax.ShapeDtypeStruct((t, n_vg), _BF16)],
        out_specs=[pl.BlockSpec((bm, bn), lambda i, j: (i, j))],
        epilogue=epilogue, name="ret_vg")[0]


def _matmul_residual(x, w, res, *, bm, bn, name):
    t = x.shape[0]
    n = w.shape[1]

    def epilogue(acc, i, j, ex, outs):
        outs[0][...] = ex[0][...] + acc

    spec = pl.BlockSpec((bm, bn), lambda i, j: (i, j))
    return _fused_matmul(
        x, w, bm=bm, bn=bn, n_col_blocks=n // bn, extras=(res,), extra_specs=(spec,),
        out_shapes=[jax.ShapeDtypeStruct((t, n), _F32)], out_specs=[spec],
        epilogue=epilogue, name=name)[0]


def _norm_matmul(h, gain, w, *, bm, bn, act, name):
    t = h.shape[0]
    n = w.shape[1]

    def epilogue(acc, i, j, ex, outs):
        outs[0][...] = act(acc).astype(_BF16)

    return _fused_matmul(
        h, w, bm=bm, bn=bn, n_col_blocks=n // bn, gain=gain,
        out_shapes=[jax.ShapeDtypeStruct((t, n), _BF16)],
        out_specs=[pl.BlockSpec((bm, bn), lambda i, j: (i, j))],
        epilogue=epilogue, name=name)[0]


def _identity(a):
    return a


def _rms_rows_inplace(o_ref, g_ref):
    g = g_ref[...]

    def chunk(c, carry):
        rows = slice(c * _NORM_ROWS, (c + 1) * _NORM_ROWS)
        x = o_ref[rows, :]
        ms = jnp.mean(x * x, axis=-1, keepdims=True)
        o_ref[rows, :] = (x * lax.rsqrt(ms + _RMS_EPS)) * g
        return carry

    for c in range(o_ref.shape[0] // _NORM_ROWS):
        chunk(c, 0)


def _mlp_body(x_ref, g_ref, wu_ref, wd_ref, o_ref, xn_ref):
    f = pl.program_id(1)

    @pl.when(f == 0)
    def _():
        _rms_rows(x_ref, g_ref, xn_ref)
        o_ref[...] = x_ref[...]

    u = jnp.dot(xn_ref[...], wu_ref[...], preferred_element_type=_F32)
    r = jnp.maximum(u, 0.0)
    o_ref[...] += jnp.dot((r * r).astype(_BF16), wd_ref[...], preferred_element_type=_F32)


def _mlp(h, gain, w_up, w_down, name):
    t, d = h.shape
    ff = w_up.shape[1]
    bm, bf = 512, 1024
    block_bytes = 2 * _nbytes((bm, d), _F32) + _nbytes((d, bf), _BF16) + _nbytes((bf, d), _BF16)
    temp_bytes = _nbytes((bm, bf), _F32) * 2 + _nbytes((bm, d), _F32)
    return pl.pallas_call(
        _mlp_body,
        grid=(t // bm, ff // bf),
        in_specs=[
            pl.BlockSpec((bm, d), lambda i, f: (i, 0)),
            pl.BlockSpec((1, d), lambda i, f: (0, 0)),
            pl.BlockSpec((d, bf), lambda i, f: (0, f)),
            pl.BlockSpec((bf, d), lambda i, f: (f, 0)),
        ],
        out_specs=pl.BlockSpec((bm, d), lambda i, f: (i, 0)),
        out_shape=jax.ShapeDtypeStruct((t, d), _F32),
        scratch_shapes=[pltpu.VMEM((bm, d), _BF16)],
        compiler_params=pltpu.CompilerParams(
            dimension_semantics=("parallel", "arbitrary"),
            vmem_limit_bytes=_vmem_limit(block_bytes, _nbytes((bm, d), _BF16), temp_bytes)),
        name=name,
    )(h, gain.reshape(1, d).astype(_F32), w_up, w_down)


def _ple_body(*refs, final):
    if final:
        x_ref, g_ref, wg_ref, p_ref, wu_ref, gf_ref, o_ref, xn_ref = refs
    else:
        x_ref, g_ref, wg_ref, p_ref, wu_ref, o_ref, xn_ref = refs
    _rms_rows(x_ref, g_ref, xn_ref)
    gate = jnp.dot(xn_ref[...], wg_ref[...], preferred_element_type=_F32)
    up = jnp.dot(p_ref[...].astype(_BF16), wu_ref[...], preferred_element_type=_F32)
    o_ref[...] = x_ref[...] + up * jax.nn.sigmoid(gate)
    if final:
        _rms_rows_inplace(o_ref, gf_ref)


def _per_layer_embed(h, p_i, gain, w_up, w_gate, name, final_gain=None):
    t, d = h.shape
    pd = p_i.shape[1]
    bm = 512
    final = final_gain is not None
    row_spec = pl.BlockSpec((bm, d), lambda i: (i, 0))
    vec_spec = pl.BlockSpec((1, d), lambda i: (0, 0))
    in_specs = [row_spec, vec_spec, pl.BlockSpec((d, d), lambda i: (0, 0)),
                pl.BlockSpec((bm, pd), lambda i: (i, 0)), pl.BlockSpec((pd, d), lambda i: (0, 0))]
    args = [h, gain.reshape(1, d).astype(_F32), w_gate, p_i, w_up]
    if final:
        in_specs.append(vec_spec)
        args.append(final_gain.reshape(1, d).astype(_F32))
    block_bytes = (2 * _nbytes((bm, d), _F32) + _nbytes((d, d), _BF16) + _nbytes((bm, pd), _F32)
                   + _nbytes((pd, d), _BF16))
    return pl.pallas_call(
        functools.partial(_ple_body, final=final),
        grid=(t // bm,),
        in_specs=in_specs,
        out_specs=row_spec,
        out_shape=jax.ShapeDtypeStruct((t, d), _F32),
        scratch_shapes=[pltpu.VMEM((bm, d), _BF16)],
        compiler_params=pltpu.CompilerParams(
            dimension_semantics=("parallel",),
            vmem_limit_bytes=_vmem_limit(block_bytes, _nbytes((bm, d), _BF16),
                                         3 * _nbytes((bm, d), _F32))),
        name=name,
    )(*args)


def _shared_kv(h, gain, w_kv):
    t, d = h.shape
    n = w_kv.shape[1]
    bm, bn = 1024, 1024
    per = bm // _MOBA_BLOCK

    def epilogue(acc, i, j, ex, outs):
        outs[0][...] = acc.astype(_BF16)
        for r in range(per):
            blk = acc[r * _MOBA_BLOCK:(r + 1) * _MOBA_BLOCK, :]
            outs[1][0, r:r + 1, :] = jnp.mean(blk, axis=0, keepdims=True)

    kv, means = _fused_matmul(
        h, w_kv, bm=bm, bn=bn, n_col_blocks=n // bn, gain=gain,
        out_shapes=[jax.ShapeDtypeStruct((t, n), _BF16),
                    jax.ShapeDtypeStruct((t // bm, per, n), _F32)],
        out_specs=[pl.BlockSpec((bm, bn), lambda i, j: (i, j)),
                   pl.BlockSpec((1, per, bn), lambda i, j: (i, 0, j))],
        epilogue=epilogue, name="shared_kv")
    return kv, means.reshape(t // _MOBA_BLOCK, n)


def _ret_body(lg_ref, q_ref, k_ref, v_ref, g_ref, o_ref, state_ref, *, chunk, n_chunks):
    hd = pl.program_id(1)

    @pl.when(pl.program_id(2) == 0)
    def _():
        state_ref[...] = jnp.zeros_like(state_ref)

    lg = lg_ref[hd]
    ri = lax.broadcasted_iota(jnp.int32, (chunk, chunk), 0)
    ci = lax.broadcasted_iota(jnp.int32, (chunk, chunk), 1)
    diff = ri - ci
    decay = jnp.where(diff >= 0, jnp.exp(jnp.maximum(diff, 0).astype(_F32) * lg), 0.0)
    r1 = lax.broadcasted_iota(jnp.int32, (chunk, 1), 0)
    xi = jnp.exp((r1 + 1).astype(_F32) * lg)
    zeta = jnp.exp((chunk - 1 - r1).astype(_F32) * lg)
    chunk_decay = jnp.exp(jnp.full((1, 1), chunk, _F32) * lg)

    def step(c, carry):
        r = pl.multiple_of(c * chunk, chunk)
        qc = q_ref[pl.ds(r, chunk), :]
        kc = k_ref[pl.ds(r, chunk), :]
        vc = v_ref[pl.ds(r, chunk), :]
        scores = lax.dot_general(qc, kc, (((1,), (1,)), ((), ())),
                                 preferred_element_type=_F32) * decay
        intra = jnp.dot(scores.astype(_BF16), vc, preferred_element_type=_F32)
        state = state_ref[...]
        inter = jnp.dot(qc, state.astype(_BF16), preferred_element_type=_F32) * xi
        kz_t = jnp.transpose(kc.astype(_F32) * zeta).astype(_BF16)
        state_ref[...] = state * chunk_decay + jnp.dot(kz_t, vc, preferred_element_type=_F32)
        y = intra + inter
        mu = jnp.mean(y, axis=-1, keepdims=True)
        yc = y - mu
        var = jnp.mean(yc * yc, axis=-1, keepdims=True)
        yn = yc * lax.rsqrt(var + _GN_EPS)
        gate = g_ref[pl.ds(r, chunk), :].astype(_F32)
        o_ref[pl.ds(r, chunk), :] = (gate * jax.nn.sigmoid(gate) * yn).astype(_BF16)
        return carry

    lax.fori_loop(0, n_chunks, step, 0, unroll=True)


def _retention_core(qk, vg, batch, seq, d):
    t = qk.shape[0]
    dk = d // _RET_HEADS
    dv = 2 * dk
    vdim = _RET_HEADS * dv
    tb = 1024
    nt = seq // tb
    log_gamma = jnp.log1p(-jnp.exp2(-5.0 - jnp.arange(_RET_HEADS, dtype=_F32)))

    def row(b, h, s):
        return b * nt + s

    block_bytes = 2 * _nbytes((tb, dk), _BF16) + 3 * _nbytes((tb, dv), _BF16)
    return pl.pallas_call(
        functools.partial(_ret_body, chunk=_RET_CHUNK, n_chunks=tb // _RET_CHUNK),
        grid=(batch, _RET_HEADS, nt),
        in_specs=[
            pl.BlockSpec(memory_space=pltpu.SMEM),
            pl.BlockSpec((tb, dk), lambda b, h, s: (row(b, h, s), h)),
            pl.BlockSpec((tb, dk), lambda b, h, s: (row(b, h, s), _RET_HEADS + h)),
            pl.BlockSpec((tb, dv), lambda b, h, s: (row(b, h, s), h)),
            pl.BlockSpec((tb, dv), lambda b, h, s: (row(b, h, s), _RET_HEADS + h)),
        ],
        out_specs=pl.BlockSpec((tb, dv), lambda b, h, s: (row(b, h, s), h)),
        out_shape=jax.ShapeDtypeStruct((t, vdim), _BF16),
        scratch_shapes=[pltpu.VMEM((dk, dv), _F32)],
        compiler_params=pltpu.CompilerParams(
            dimension_semantics=("parallel", "parallel", "arbitrary"),
            vmem_limit_bytes=_vmem_limit(block_bytes, _nbytes((dk, dv), _F32), 8 << 20)),
        name="retention_core",
    )(log_gamma, qk, qk, vg, vg)


def _t5_thresholds():
    max_exact = _REL_BUCKETS // 2
    n_log = _REL_BUCKETS - max_exact
    ratio = _REL_MAX_DIST / max_exact
    thr = list(range(max_exact + 1))
    for kk in range(1, n_log):
        guess = int(math.floor(max_exact * ratio ** (kk / n_log)))
        n = max(guess - 2, max_exact)
        while not (n / max_exact) ** n_log >= ratio ** kk:
            n += 1
        thr.append(n)
    return thr


def _moba_body(rb_ref, q_ref, k_ref, v_ref, km_ref, o_ref, bias_ref, vt_ref, *,
               blk, nblk, n_tiles, thresholds, scale, topk, group):
    hd = pl.program_id(0)
    gstep = pl.program_id(2)

    @pl.when((pl.program_id(1) == 0) & (gstep == 0))
    def _():
        ki = lax.broadcasted_iota(jnp.int32, (blk, blk), 0)
        qi = lax.broadcasted_iota(jnp.int32, (blk, blk), 1)

        def build(dd, carry):
            dist = qi - ki + dd * blk
            val = jnp.full((blk, blk), rb_ref[0, hd], _F32)
            for b in range(1, _REL_BUCKETS):
                val = jnp.where(dist >= thresholds[b], rb_ref[b, hd], val)
            bias_ref[dd] = jnp.where(dist >= 0, val * _LOG2E, _NEG_INF)
            return carry

        lax.fori_loop(0, n_tiles, build, 0)
        bias_ref[n_tiles] = jnp.full((blk, blk), rb_ref[_REL_BUCKETS - 1, hd] * _LOG2E, _F32)

    sub = 8

    @pl.when(gstep == 0)
    def _():
        dh = v_ref.shape[1]
        vt_ref[dh:dh + 16, :] = jnp.ones((16, vt_ref.shape[1]), _BF16)
        for n in range(nblk):
            v_blk = v_ref[n * blk:(n + 1) * blk, :].astype(_F32)
            vt_ref[0:dh, n * blk:(n + 1) * blk] = jnp.transpose(v_blk).astype(_BF16)

    km = km_ref[...].astype(_BF16)
    row = lax.broadcasted_iota(jnp.int32, (nblk, blk), 0)

    def attend_query_block(qb, r):
        q = q_ref[r * blk:(r + 1) * blk, :]
        gate = lax.dot_general(km, q, (((1,), (1,)), ((), ())), preferred_element_type=_F32)
        past = row < qb
        g = jnp.where(past, gate, _NEG_INF)
        sel = jnp.zeros((nblk, blk), jnp.bool_)
        for _ in range(topk):
            mx = jnp.max(g, axis=0, keepdims=True)
            idx = jnp.min(jnp.where(g == mx, row, nblk), axis=0, keepdims=True)
            pick = row == idx
            sel = jnp.logical_or(sel, pick)
            g = jnp.where(pick, -jnp.inf, g)
        attend = jnp.logical_or(jnp.logical_and(sel, past), row == qb)
        pen = jnp.where(attend, 0.0, _NEG_INF).astype(_F32)
        s = []
        m_part = None
        for n in range(qb + 1):
            kq = lax.dot_general(k_ref[n * blk:(n + 1) * blk, :], q, (((1,), (1,)), ((), ())),
                                 preferred_element_type=_F32)
            sn = kq * (scale * _LOG2E) + bias_ref[min(qb - n, n_tiles)]
            s.append(sn)
            part = jnp.max(sn.reshape(blk // sub, sub, blk), axis=0) + pen[n:n + 1, :]
            m_part = part if m_part is None else jnp.maximum(m_part, part)
        m = jnp.max(m_part, axis=0, keepdims=True)
        acc = None
        for n in range(qb + 1):
            p = jnp.exp2(s[n] - (m - pen[n:n + 1, :]))
            pv = jnp.dot(vt_ref[:, n * blk:(n + 1) * blk], p.astype(_BF16),
                         preferred_element_type=_F32)
            acc = pv if acc is None else acc + pv
        dh = q.shape[1]
        l = acc[dh:dh + 1, :]
        o_ref[r * blk:(r + 1) * blk, :] = jnp.transpose(acc[0:dh] / l).astype(_BF16)

    for gi in range(nblk // group):
        @pl.when(gstep == gi)
        def _(gi=gi):
            for r in range(group):
                attend_query_block(gi * group + r, r)


_MOBA_GROUP = 4


def _moba_attention(q, kv, k_means, rel_bias, batch, seq):
    t, d = q.shape
    dh = d // _ATT_HEADS
    blk = _MOBA_BLOCK
    nblk = seq // blk
    thresholds = _t5_thresholds()
    n_tiles = 0
    while n_tiles * blk - (blk - 1) < thresholds[_REL_BUCKETS - 1]:
        n_tiles += 1
    n_tiles = min(n_tiles, nblk)
    group = math.gcd(_MOBA_GROUP, nblk)
    steps = nblk // group
    block_bytes = (2 * _nbytes((group * blk, dh), _BF16) + 2 * _nbytes((seq, dh), _BF16)
                   + _nbytes((nblk, dh), _F32))
    scratch_bytes = (n_tiles + 1) * blk * blk * 4 + _nbytes((dh + 16, seq), _BF16)
    return pl.pallas_call(
        functools.partial(_moba_body, blk=blk, nblk=nblk, n_tiles=n_tiles, thresholds=thresholds,
                          scale=dh ** -0.5, topk=min(_MOBA_TOPK, nblk), group=group),
        grid=(_ATT_HEADS, batch, steps),
        in_specs=[
            pl.BlockSpec(memory_space=pltpu.SMEM),
            pl.BlockSpec((group * blk, dh), lambda h, b, s: (b * steps + s, h)),
            pl.BlockSpec((seq, dh), lambda h, b, s: (b, h)),
            pl.BlockSpec((seq, dh), lambda h, b, s: (b, _ATT_HEADS + h)),
            pl.BlockSpec((nblk, dh), lambda h, b, s: (b, h)),
        ],
        out_specs=pl.BlockSpec((group * blk, dh), lambda h, b, s: (b * steps + s, h)),
        out_shape=jax.ShapeDtypeStruct((t, d), _BF16),
        scratch_shapes=[
            pltpu.VMEM((n_tiles + 1, blk, blk), _F32),
            pltpu.VMEM((dh + 16, seq), _BF16),
        ],
        compiler_params=pltpu.CompilerParams(
            dimension_semantics=("arbitrary", "arbitrary", "arbitrary"),
            vmem_limit_bytes=_vmem_limit(block_bytes, scratch_bytes, 16 << 20)),
        name="moba_attention",
    )(rel_bias.astype(_F32), q, kv, kv, k_means)


def _rotary_tables(seq, half):
    inv = _ROPE_BASE ** (-jnp.arange(half, dtype=_F32) / half)
    ang = jnp.arange(seq).astype(_F32)[:, None] * inv[None, :]
    return jnp.cos(ang), jnp.sin(ang)


def kernel(x, p, ret_norm_g, ret_w_in, ret_w_out, kv_norm_g, w_kv, att_norm_g, att_w_q, att_w_o,
           rel_bias, mlp_norm_g, mlp_w_up, mlp_w_down, ple_norm_g, ple_w_up, ple_w_gate,
           final_norm_g):
    batch, seq, d = x.shape
    depth = p.shape[0]
    n_ret = ret_w_in.shape[0]
    t = batch * seq
    h = x.reshape(t, d)
    p2 = p.reshape(depth, t, p.shape[-1])
    cos, sin = _rotary_tables(seq, (d // _RET_HEADS) // 2)
    kv = k_means = None
    for i in range(depth):
        if i < n_ret:
            w_in = ret_w_in[i].astype(_BF16)
            qk = _retention_qk(h, ret_norm_g[i], w_in, cos, sin, seq)
            vg = _retention_vg(h, ret_norm_g[i], w_in)
            y = _retention_core(qk, vg, batch, seq, d)
            h = _matmul_residual(y, ret_w_out[i].astype(_BF16), h, bm=512, bn=1024, name="ret_out")
        else:
            j = i - n_ret
            q = _norm_matmul(h, att_norm_g[j], att_w_q[j].astype(_BF16), bm=1024, bn=1024,
                             act=_identity, name="att_q")
            o = _moba_attention(q, kv, k_means, rel_bias, batch, seq)
            h = _matmul_residual(o, att_w_o[j].astype(_BF16), h, bm=1024, bn=1024, name="att_out")
        h = _mlp(h, mlp_norm_g[i], mlp_w_up[i].astype(_BF16), mlp_w_down[i].astype(_BF16),
                 name=f"mlp{i}")
        h = _per_layer_embed(h, p2[i], ple_norm_g[i], ple_w_up[i].astype(_BF16),
                             ple_w_gate[i].astype(_BF16), name=f"ple{i}",
                             final_gain=final_norm_g if i == depth - 1 else None)
        if i == n_ret - 1:
            kv, k_means = _shared_kv(h, kv_norm_g, w_kv.astype(_BF16))
    return h.reshape(batch, seq, d)
```

```python
import functools
import math

import jax
import jax.numpy as jnp
from jax import lax
from jax.experimental import pallas as pl
from jax.experimental.pallas import tpu as pltpu

_F32 = jnp.float32
_BF16 = jnp.bfloat16

_RET_HEADS = 8
_RET_CHUNK = 128
_ROPE_BASE = 10000.0
_GN_EPS = 1e-5
_ATT_HEADS = 16
_MOBA_BLOCK = 256
_MOBA_TOPK = 3
_REL_BUCKETS = 32
_REL_MAX_DIST = 1024
_RMS_EPS = 1e-6
_NEG_INF = -1e30
_LOG2E = math.log2(math.e)

_V7X_VMEM_BUDGET = 60 * 1024 * 1024
_NORM_ROWS = 32
_NORM_UNROLL = 4


def _vmem_limit(block_bytes, scratch_bytes=0, temp_bytes=0):
    need = 2 * block_bytes + scratch_bytes + temp_bytes + (4 << 20)
    return int(min(max(need, 16 << 20), _V7X_VMEM_BUDGET))


def _nbytes(shape, dtype):
    return math.prod(shape) * jnp.dtype(dtype).itemsize


def _rms_rows(x_ref, g_ref, xn_ref):
    g = g_ref[...]

    def chunk(c, carry):
        r = pl.multiple_of(c * _NORM_ROWS, _NORM_ROWS)
        x = x_ref[pl.ds(r, _NORM_ROWS), :].astype(_F32)
        ms = jnp.mean(x * x, axis=-1, keepdims=True)
        xn_ref[pl.ds(r, _NORM_ROWS), :] = ((x * lax.rsqrt(ms + _RMS_EPS)) * g).astype(_BF16)
        return carry

    lax.fori_loop(0, x_ref.shape[0] // _NORM_ROWS, chunk, 0, unroll=_NORM_UNROLL)


def _mm_body(*refs, has_norm, n_extra, n_out, epilogue):
    x_ref = refs[0]
    pos = 1
    g_ref = None
    if has_norm:
        g_ref = refs[pos]
        pos += 1
    w_ref = refs[pos]
    pos += 1
    extra_refs = refs[pos:pos + n_extra]
    pos += n_extra
    out_refs = refs[pos:pos + n_out]
    pos += n_out
    i = pl.program_id(0)
    j = pl.program_id(1)
    if has_norm:
        xn_ref = refs[pos]

        @pl.when(j == 0)
        def _():
            _rms_rows(x_ref, g_ref, xn_ref)

        lhs = xn_ref[...]
    else:
        lhs = x_ref[...]
    acc = jnp.dot(lhs, w_ref[...], preferred_element_type=_F32)
    epilogue(acc, i, j, extra_refs, out_refs)


def _fused_matmul(x, w, *, bm, bn, n_col_blocks, w_col_block_off=0, gain=None, extras=(),
                  extra_specs=(), out_shapes, out_specs, epilogue, name):
    m, k = x.shape
    assert m % bm == 0 and w.shape[0] == k and w.shape[1] % bn == 0
    has_norm = gain is not None
    in_specs = [pl.BlockSpec((bm, k), lambda i, j: (i, 0))]
    args = [x]
    if has_norm:
        in_specs.append(pl.BlockSpec((1, k), lambda i, j: (0, 0)))
        args.append(gain.reshape(1, k).astype(_F32))
    in_specs.append(pl.BlockSpec((k, bn), lambda i, j: (0, j + w_col_block_off)))
    args.append(w)
    in_specs.extend(extra_specs)
    args.extend(extras)
    scratch = [pltpu.VMEM((bm, k), _BF16)] if has_norm else []
    block_bytes = _nbytes((bm, k), x.dtype) + _nbytes((k, bn), w.dtype)
    for spec, arr in zip(extra_specs, extras):
        block_bytes += _nbytes([d for d in spec.block_shape if d is not None], arr.dtype)
    for spec, sh in zip(out_specs, out_shapes):
        block_bytes += _nbytes([d for d in spec.block_shape if d is not None], sh.dtype)
    scratch_bytes = _nbytes((bm, k), _BF16) if has_norm else 0
    body = functools.partial(_mm_body, has_norm=has_norm, n_extra=len(extras),
                             n_out=len(out_shapes), epilogue=epilogue)
    return pl.pallas_call(
        body,
        grid=(m // bm, n_col_blocks),
        in_specs=in_specs,
        out_specs=list(out_specs),
        out_shape=list(out_shapes),
        scratch_shapes=scratch,
        compiler_params=pltpu.CompilerParams(
            dimension_semantics=("parallel", "arbitrary"),
            vmem_limit_bytes=_vmem_limit(block_bytes, scratch_bytes, 3 * bm * bn * 4)),
        name=name,
    )(*args)


def _retention_qk(h, gain, w_in, cos, sin, seq):
    t, d = h.shape
    bm, bn = 1024, 1024
    dk = d // _RET_HEADS
    half = dk // 2
    n_q_blocks = d // bn
    k_scale = dk ** -0.5

    def epilogue(acc, i, j, ex, outs):
        o_ref = outs[0]
        cos_b = ex[0][...]
        sin_b = ex[1][...]
        scale = jnp.where(j < n_q_blocks, 1.0, k_scale).astype(_F32)
        for c in range(bn // dk):
            x1 = acc[:, c * dk:c * dk + half]
            x2 = acc[:, c * dk + half:(c + 1) * dk]
            o_ref[:, c * dk:c * dk + half] = ((x1 * cos_b - x2 * sin_b) * scale).astype(_BF16)
            o_ref[:, c * dk + half:(c + 1) * dk] = ((x1 * sin_b + x2 * cos_b) * scale).astype(_BF16)

    pos_blocks = seq // bm
    tab_spec = pl.BlockSpec((bm, half), lambda i, j: (i % pos_blocks, 0))
    return _fused_matmul(
        h, w_in, bm=bm, bn=bn, n_col_blocks=(2 * d) // bn, gain=gain,
        extras=(cos, sin), extra_specs=(tab_spec, tab_spec),
        out_shapes=[jax.ShapeDtypeStruct((t, 2 * d), _BF16)],
        out_specs=[pl.BlockSpec((bm, bn), lambda i, j: (i, j))],
        epilogue=epilogue, name="ret_qk")[0]


def _retention_vg(h, gain, w_in):
    t, d = h.shape
    bm, bn = 1024, 1024
    n_vg = w_in.shape[1] - 2 * d

    def epilogue(acc, i, j, ex, outs):
        outs[0][...] = acc.astype(_BF16)

    return _fused_matmul(
        h, w_in, bm=bm, bn=bn, n_col_blocks=n_vg // bn, w_col_block_off=(2 * d) // bn, gain=gain,
        out_shapes=[jax.ShapeDtypeStruct((t, n_vg), _BF16)],
        out_specs=[pl.BlockSpec((bm, bn), lambda i, j: (i, j))],
        epilogue=epilogue, name="ret_vg")[0]


def _matmul_residual(x, w, res, *, bm, bn, name):
    t = x.shape[0]
    n = w.shape[1]

    def epilogue(acc, i, j, ex, outs):
        outs[0][...] = ex[0][...] + acc

    spec = pl.BlockSpec((bm, bn), lambda i, j: (i, j))
    return _fused_matmul(
        x, w, bm=bm, bn=bn, n_col_blocks=n // bn, extras=(res,), extra_specs=(spec,),
        out_shapes=[jax.ShapeDtypeStruct((t, n), _F32)], out_specs=[spec],
        epilogue=epilogue, name=name)[0]


def _norm_matmul(h, gain, w, *, bm, bn, act, name):
    t = h.shape[0]
    n = w.shape[1]

    def epilogue(acc, i, j, ex, outs):
        outs[0][...] = act(acc).astype(_BF16)

    return _fused_matmul(
        h, w, bm=bm, bn=bn, n_col_blocks=n // bn, gain=gain,
        out_shapes=[jax.ShapeDtypeStruct((t, n), _BF16)],
        out_specs=[pl.BlockSpec((bm, bn), lambda i, j: (i, j))],
        epilogue=epilogue, name=name)[0]


def _identity(a):
    return a


def _rms_rows_inplace(o_ref, g_ref):
    g = g_ref[...]

    def chunk(c, carry):
        rows = slice(c * _NORM_ROWS, (c + 1) * _NORM_ROWS)
        x = o_ref[rows, :]
        ms = jnp.mean(x * x, axis=-1, keepdims=True)
        o_ref[rows, :] = (x * lax.rsqrt(ms + _RMS_EPS)) * g
        return carry

    for c in range(o_ref.shape[0] // _NORM_ROWS):
        chunk(c, 0)


def _mlp_body(x_ref, g_ref, wu_ref, wd_ref, o_ref, xn_ref):
    f = pl.program_id(1)

    @pl.when(f == 0)
    def _():
        _rms_rows(x_ref, g_ref, xn_ref)
        o_ref[...] = x_ref[...]

    u = jnp.dot(xn_ref[...], wu_ref[...], preferred_element_type=_F32)
    r = jnp.maximum(u, 0.0)
    o_ref[...] += jnp.dot((r * r).astype(_BF16), wd_ref[...], preferred_element_type=_F32)


def _mlp(h, gain, w_up, w_down, name):
    t, d = h.shape
    ff = w_up.shape[1]
    bm, bf = 512, 1024
    block_bytes = 2 * _nbytes((bm, d), _F32) + _nbytes((d, bf), _BF16) + _nbytes((bf, d), _BF16)
    temp_bytes = _nbytes((bm, bf), _F32) * 2 + _nbytes((bm, d), _F32)
    return pl.pallas_call(
        _mlp_body,
        grid=(t // bm, ff // bf),
        in_specs=[
            pl.BlockSpec((bm, d), lambda i, f: (i, 0)),
            pl.BlockSpec((1, d), lambda i, f: (0, 0)),
            pl.BlockSpec((d, bf), lambda i, f: (0, f)),
            pl.BlockSpec((bf, d), lambda i, f: (f, 0)),
        ],
        out_specs=pl.BlockSpec((bm, d), lambda i, f: (i, 0)),
        out_shape=jax.ShapeDtypeStruct((t, d), _F32),
        scratch_shapes=[pltpu.VMEM((bm, d), _BF16)],
        compiler_params=pltpu.CompilerParams(
            dimension_semantics=("parallel", "arbitrary"),
            vmem_limit_bytes=_vmem_limit(block_bytes, _nbytes((bm, d), _BF16), temp_bytes)),
        name=name,
    )(h, gain.reshape(1, d).astype(_F32), w_up, w_down)


def _ple_body(*refs, final):
    if final:
        x_ref, g_ref, wg_ref, p_ref, wu_ref, gf_ref, o_ref, xn_ref = refs
    else:
        x_ref, g_ref, wg_ref, p_ref, wu_ref, o_ref, xn_ref = refs
    _rms_rows(x_ref, g_ref, xn_ref)
    gate = jnp.dot(xn_ref[...], wg_ref[...], preferred_element_type=_F32)
    up = jnp.dot(p_ref[...].astype(_BF16), wu_ref[...], preferred_element_type=_F32)
    o_ref[...] = x_ref[...] + up * jax.nn.sigmoid(gate)
    if final:
        _rms_rows_inplace(o_ref, gf_ref)


def _per_layer_embed(h, p_i, gain, w_up, w_gate, name, final_gain=None):
    t, d = h.shape
    pd = p_i.shape[1]
    bm = 512
    final = final_gain is not None
    row_spec = pl.BlockSpec((bm, d), lambda i: (i, 0))
    vec_spec = pl.BlockSpec((1, d), lambda i: (0, 0))
    in_specs = [row_spec, vec_spec, pl.BlockSpec((d, d), lambda i: (0, 0)),
                pl.BlockSpec((bm, pd), lambda i: (i, 0)), pl.BlockSpec((pd, d), lambda i: (0, 0))]
    args = [h, gain.reshape(1, d).astype(_F32), w_gate, p_i, w_up]
    if final:
        in_specs.append(vec_spec)
        args.append(final_gain.reshape(1, d).astype(_F32))
    block_bytes = (2 * _nbytes((bm, d), _F32) + _nbytes((d, d), _BF16) + _nbytes((bm, pd), _F32)
                   + _nbytes((pd, d), _BF16))
    return pl.pallas_call(
        functools.partial(_ple_body, final=final),
        grid=(t // bm,),
        in_specs=in_specs,
        out_specs=row_spec,
        out_shape=jax.ShapeDtypeStruct((t, d), _F32),
        scratch_shapes=[pltpu.VMEM((bm, d), _BF16)],
        compiler_params=pltpu.CompilerParams(
            dimension_semantics=("parallel",),
            vmem_limit_bytes=_vmem_limit(block_bytes, _nbytes((bm, d), _BF16),
                                         3 * _nbytes((bm, d), _F32))),
        name=name,
    )(*args)


def _shared_kv(h, gain, w_kv):
    t, d = h.shape
    n = w_kv.shape[1]
    bm, bn = 1024, 1024
    per = bm // _MOBA_BLOCK

    def epilogue(acc, i, j, ex, outs):
        outs[0][...] = acc.astype(_BF16)
        for r in range(per):
            blk = acc[r * _MOBA_BLOCK:(r + 1) * _MOBA_BLOCK, :]
            outs[1][0, r:r + 1, :] = jnp.mean(blk, axis=0, keepdims=True)

    kv, means = _fused_matmul(
        h, w_kv, bm=bm, bn=bn, n_col_blocks=n // bn, gain=gain,
        out_shapes=[jax.ShapeDtypeStruct((t, n), _BF16),
                    jax.ShapeDtypeStruct((t // bm, per, n), _F32)],
        out_specs=[pl.BlockSpec((bm, bn), lambda i, j: (i, j)),
                   pl.BlockSpec((1, per, bn), lambda i, j: (i, 0, j))],
        epilogue=epilogue, name="shared_kv")
    return kv, means.reshape(t // _MOBA_BLOCK, n)


def _ret_body(lg_ref, q_ref, k_ref, v_ref, g_ref, o_ref, state_ref, *, chunk, n_chunks):
    hd = pl.program_id(1)

    @pl.when(pl.program_id(2) == 0)
    def _():
        state_ref[...] = jnp.zeros_like(state_ref)

    lg = lg_ref[hd]
    ri = lax.broadcasted_iota(jnp.int32, (chunk, chunk), 0)
    ci = lax.broadcasted_iota(jnp.int32, (chunk, chunk), 1)
    diff = ri - ci
    decay = jnp.where(diff >= 0, jnp.exp(jnp.maximum(diff, 0).astype(_F32) * lg), 0.0)
    r1 = lax.broadcasted_iota(jnp.int32, (chunk, 1), 0)
    xi = jnp.exp((r1 + 1).astype(_F32) * lg)
    zeta = jnp.exp((chunk - 1 - r1).astype(_F32) * lg)
    chunk_decay = jnp.exp(jnp.full((1, 1), chunk, _F32) * lg)

    def step(c, carry):
        r = pl.multiple_of(c * chunk, chunk)
        qc = q_ref[pl.ds(r, chunk), :]
        kc = k_ref[pl.ds(r, chunk), :]
        vc = v_ref[pl.ds(r, chunk), :]
        scores = lax.dot_general(qc, kc, (((1,), (1,)), ((), ())),
                                 preferred_element_type=_F32) * decay
        intra = jnp.dot(scores.astype(_BF16), vc, preferred_element_type=_F32)
        state = state_ref[...]
        inter = jnp.dot(qc, state.astype(_BF16), preferred_element_type=_F32) * xi
        kz_t = jnp.transpose(kc.astype(_F32) * zeta).astype(_BF16)
        state_ref[...] = state * chunk_decay + jnp.dot(kz_t, vc, preferred_element_type=_F32)
        y = intra + inter
        mu = jnp.mean(y, axis=-1, keepdims=True)
        yc = y - mu
        var = jnp.mean(yc * yc, axis=-1, keepdims=True)
        yn = yc * lax.rsqrt(var + _GN_EPS)
        gate = g_ref[pl.ds(r, chunk), :].astype(_F32)
        o_ref[pl.ds(r, chunk), :] = (gate * jax.nn.sigmoid(gate) * yn).astype(_BF16)
        return carry

    lax.fori_loop(0, n_chunks, step, 0, unroll=True)


def _retention_core(qk, vg, batch, seq, d):
    t = qk.shape[0]
    dk = d // _RET_HEADS
    dv = 2 * dk
    vdim = _RET_HEADS * dv
    tb = 1024
    nt = seq // tb
    log_gamma = jnp.log1p(-jnp.exp2(-5.0 - jnp.arange(_RET_HEADS, dtype=_F32)))

    def row(b, h, s):
        return b * nt + s

    block_bytes = 2 * _nbytes((tb, dk), _BF16) + 3 * _nbytes((tb, dv), _BF16)
    return pl.pallas_call(
        functools.partial(_ret_body, chunk=_RET_CHUNK, n_chunks=tb // _RET_CHUNK),
        grid=(batch, _RET_HEADS, nt),
        in_specs=[
            pl.BlockSpec(memory_space=pltpu.SMEM),
            pl.BlockSpec((tb, dk), lambda b, h, s: (row(b, h, s), h)),
            pl.BlockSpec((tb, dk), lambda b, h, s: (row(b, h, s), _RET_HEADS + h)),
            pl.BlockSpec((tb, dv), lambda b, h, s: (row(b, h, s), h)),
            pl.BlockSpec((tb, dv), lambda b, h, s: (row(b, h, s), _RET_HEADS + h)),
        ],
        out_specs=pl.BlockSpec((tb, dv), lambda b, h, s: (row(b, h, s), h)),
        out_shape=jax.ShapeDtypeStruct((t, vdim), _BF16),
        scratch_shapes=[pltpu.VMEM((dk, dv), _F32)],
        compiler_params=pltpu.CompilerParams(
            dimension_semantics=("parallel", "parallel", "arbitrary"),
            vmem_limit_bytes=_vmem_limit(block_bytes, _nbytes((dk, dv), _F32), 8 << 20)),
        name="retention_core",
    )(log_gamma, qk, qk, vg, vg)


def _t5_thresholds():
    max_exact = _REL_BUCKETS // 2
    n_log = _REL_BUCKETS - max_exact
    ratio = _REL_MAX_DIST / max_exact
    thr = list(range(max_exact + 1))
    for kk in range(1, n_log):
        guess = int(math.floor(max_exact * ratio ** (kk / n_log)))
        n = max(guess - 2, max_exact)
        while not (n / max_exact) ** n_log >= ratio ** kk:
            n += 1
        thr.append(n)
    return thr


def _moba_body(rb_ref, q_ref, k_ref, v_ref, km_ref, o_ref, bias_ref, vt_ref, *,
               blk, nblk, n_tiles, thresholds, scale, topk, group):
    hd = pl.program_id(0)
    gstep = pl.program_id(2)

    @pl.when((pl.program_id(1) == 0) & (gstep == 0))
    def _():
        ki = lax.broadcasted_iota(jnp.int32, (blk, blk), 0)
        qi = lax.broadcasted_iota(jnp.int32, (blk, blk), 1)

        def build(dd, carry):
            dist = qi - ki + dd * blk
            val = jnp.full((blk, blk), rb_ref[0, hd], _F32)
            for b in range(1, _REL_BUCKETS):
                val = jnp.where(dist >= thresholds[b], rb_ref[b, hd], val)
            bias_ref[dd] = jnp.where(dist >= 0, val * _LOG2E, _NEG_INF)
            return carry

        lax.fori_loop(0, n_tiles, build, 0)
        bias_ref[n_tiles] = jnp.full((blk, blk), rb_ref[_REL_BUCKETS - 1, hd] * _LOG2E, _F32)

    sub = 8

    @pl.when(gstep == 0)
    def _():
        dh = v_ref.shape[1]
        vt_ref[dh:dh + 16, :] = jnp.ones((16, vt_ref.shape[1]), _BF16)
        for n in range(nblk):
            v_blk = v_ref[n * blk:(n + 1) * blk, :].astype(_F32)
            vt_ref[0:dh, n * blk:(n + 1) * blk] = jnp.transpose(v_blk).astype(_BF16)

    km = km_ref[...].astype(_BF16)
    row = lax.broadcasted_iota(jnp.int32, (nblk, blk), 0)

    def attend_query_block(qb, r):
        q = q_ref[r * blk:(r + 1) * blk, :]
        gate = lax.dot_general(km, q, (((1,), (1,)), ((), ())), preferred_element_type=_F32)
        past = row < qb
        g = jnp.where(past, gate, _NEG_INF)
        sel = jnp.zeros((nblk, blk), jnp.bool_)
        for _ in range(topk):
            mx = jnp.max(g, axis=0, keepdims=True)
            idx = jnp.min(jnp.where(g == mx, row, nblk), axis=0, keepdims=True)
            pick = row == idx
            sel = jnp.logical_or(sel, pick)
            g = jnp.where(pick, -jnp.inf, g)
        attend = jnp.logical_or(jnp.logical_and(sel, past), row == qb)
        pen = jnp.where(attend, 0.0, _NEG_INF).astype(_F32)
        s = []
        m_part = None
        for n in range(qb + 1):
            kq = lax.dot_general(k_ref[n * blk:(n + 1) * blk, :], q, (((1,), (1,)), ((), ())),
                                 preferred_element_type=_F32)
            sn = kq * (scale * _LOG2E) + bias_ref[min(qb - n, n_tiles)]
            s.append(sn)
            part = jnp.max(sn.reshape(blk // sub, sub, blk), axis=0) + pen[n:n + 1, :]
            m_part = part if m_part is None else jnp.maximum(m_part, part)
        m = jnp.max(m_part, axis=0, keepdims=True)
        acc = None
        for n in range(qb + 1):
            p = jnp.exp2(s[n] - (m - pen[n:n + 1, :]))
            pv = jnp.dot(vt_ref[:, n * blk:(n + 1) * blk], p.astype(_BF16),
                         preferred_element_type=_F32)
            acc = pv if acc is None else acc + pv
        dh = q.shape[1]
        l = acc[dh:dh + 1, :]
        o_ref[r * blk:(r + 1) * blk, :] = jnp.transpose(acc[0:dh] / l).astype(_BF16)

    for gi in range(nblk // group):
        @pl.when(gstep == gi)
        def _(gi=gi):
            for r in range(group):
                attend_query_block(gi * group + r, r)


_MOBA_GROUP = 16


def _moba_attention(q, kv, k_means, rel_bias, batch, seq):
    t, d = q.shape
    dh = d // _ATT_HEADS
    blk = _MOBA_BLOCK
    nblk = seq // blk
    thresholds = _t5_thresholds()
    n_tiles = 0
    while n_tiles * blk - (blk - 1) < thresholds[_REL_BUCKETS - 1]:
        n_tiles += 1
    n_tiles = min(n_tiles, nblk)
    group = math.gcd(_MOBA_GROUP, nblk)
    steps = nblk // group
    block_bytes = (2 * _nbytes((group * blk, dh), _BF16) + 2 * _nbytes((seq, dh), _BF16)
                   + _nbytes((nblk, dh), _F32))
    scratch_bytes = (n_tiles + 1) * blk * blk * 4 + _nbytes((dh + 16, seq), _BF16)
    return pl.pallas_call(
        functools.partial(_moba_body, blk=blk, nblk=nblk, n_tiles=n_tiles, thresholds=thresholds,
                          scale=dh ** -0.5, topk=min(_MOBA_TOPK, nblk), group=group),
        grid=(_ATT_HEADS, batch, steps),
        in_specs=[
            pl.BlockSpec(memory_space=pltpu.SMEM),
            pl.BlockSpec((group * blk, dh), lambda h, b, s: (b * steps + s, h)),
            pl.BlockSpec((seq, dh), lambda h, b, s: (b, h)),
            pl.BlockSpec((seq, dh), lambda h, b, s: (b, _ATT_HEADS + h)),
            pl.BlockSpec((nblk, dh), lambda h, b, s: (b, h)),
        ],
        out_specs=pl.BlockSpec((group * blk, dh), lambda h, b, s: (b * steps + s, h)),
        out_shape=jax.ShapeDtypeStruct((t, d), _BF16),
        scratch_shapes=[
            pltpu.VMEM((n_tiles + 1, blk, blk), _F32),
            pltpu.VMEM((dh + 16, seq), _BF16),
        ],
        compiler_params=pltpu.CompilerParams(
            dimension_semantics=("arbitrary", "arbitrary", "arbitrary"),
            vmem_limit_bytes=_vmem_limit(block_bytes, scratch_bytes, 16 << 20)),
        name="moba_attention",
    )(rel_bias.astype(_F32), q, kv, kv, k_means)


def _rotary_tables(seq, half):
    inv = _ROPE_BASE ** (-jnp.arange(half, dtype=_F32) / half)
    ang = jnp.arange(seq).astype(_F32)[:, None] * inv[None, :]
    return jnp.cos(ang), jnp.sin(ang)


def kernel(x, p, ret_norm_g, ret_w_in, ret_w_out, kv_norm_g, w_kv, att_norm_g, att_w_q, att_w_o,
           rel_bias, mlp_norm_g, mlp_w_up, mlp_w_down, ple_norm_g, ple_w_up, ple_w_gate,
           final_norm_g):
    batch, seq, d = x.shape
    depth = p.shape[0]
    n_ret = ret_w_in.shape[0]
    t = batch * seq
    h = x.reshape(t, d)
    p2 = p.reshape(depth, t, p.shape[-1])
    cos, sin = _rotary_tables(seq, (d // _RET_HEADS) // 2)
    kv = k_means = None
    for i in range(depth):
        if i < n_ret:
            w_in = ret_w_in[i].astype(_BF16)
            qk = _retention_qk(h, ret_norm_g[i], w_in, cos, sin, seq)
            vg = _retention_vg(h, ret_norm_g[i], w_in)
            y = _retention_core(qk, vg, batch, seq, d)
            h = _matmul_residual(y, ret_w_out[i].astype(_BF16), h, bm=512, bn=1024, name="ret_out")
        else:
            j = i - n_ret
            q = _norm_matmul(h, att_norm_g[j], att_w_q[j].astype(_BF16), bm=1024, bn=1024,
                             act=_identity, name="att_q")
            o = _moba_attention(q, kv, k_means, rel_bias, batch, seq)
            h = _matmul_residual(o, att_w_o[j].astype(_BF16), h, bm=1024, bn=1024, name="att_out")
        h = _mlp(h, mlp_norm_g[i], mlp_w_up[i].astype(_BF16), mlp_w_down[i].astype(_BF16),
                 name=f"mlp{i}")
        h = _per_layer_embed(h, p2[i], ple_norm_g[i], ple_w_up[i].astype(_BF16),
                             ple_w_gate[i].astype(_BF16), name=f"ple{i}",
                             final_gain=final_norm_g if i == depth - 1 else None)
        if i == n_ret - 1:
            kv, k_means = _shared_kv(h, kv_norm_g, w_kv.astype(_BF16))
    return h.reshape(batch, seq, d)
```

```python
import functools
import math

import jax
import jax.numpy as jnp
from jax import lax
from jax.experimental import pallas as pl
from jax.experimental.pallas import tpu as pltpu

_F32 = jnp.float32
_BF16 = jnp.bfloat16

_RET_HEADS = 8
_RET_CHUNK = 128
_ROPE_BASE = 10000.0
_GN_EPS = 1e-5
_ATT_HEADS = 16
_MOBA_BLOCK = 256
_MOBA_TOPK = 3
_REL_BUCKETS = 32
_REL_MAX_DIST = 1024
_RMS_EPS = 1e-6
_NEG_INF = -1e30
_LOG2E = math.log2(math.e)

_V7X_VMEM_BUDGET = 60 * 1024 * 1024
_NORM_ROWS = 32
_NORM_UNROLL = 4


def _vmem_limit(block_bytes, scratch_bytes=0, temp_bytes=0):
    need = 2 * block_bytes + scratch_bytes + temp_bytes + (4 << 20)
    return int(min(max(need, 16 << 20), _V7X_VMEM_BUDGET))


def _nbytes(shape, dtype):
    return math.prod(shape) * jnp.dtype(dtype).itemsize


def _rms_rows(x_ref, g_ref, xn_ref):
    g = g_ref[...]

    def chunk(c, carry):
        r = pl.multiple_of(c * _NORM_ROWS, _NORM_ROWS)
        x = x_ref[pl.ds(r, _NORM_ROWS), :].astype(_F32)
        ms = jnp.mean(x * x, axis=-1, keepdims=True)
        xn_ref[pl.ds(r, _NORM_ROWS), :] = ((x * lax.rsqrt(ms + _RMS_EPS)) * g).astype(_BF16)
        return carry

    lax.fori_loop(0, x_ref.shape[0] // _NORM_ROWS, chunk, 0, unroll=_NORM_UNROLL)


def _mm_body(*refs, has_norm, n_extra, n_out, epilogue):
    x_ref = refs[0]
    pos = 1
    g_ref = None
    if has_norm:
        g_ref = refs[pos]
        pos += 1
    w_ref = refs[pos]
    pos += 1
    extra_refs = refs[pos:pos + n_extra]
    pos += n_extra
    out_refs = refs[pos:pos + n_out]
    pos += n_out
    i = pl.program_id(0)
    j = pl.program_id(1)
    if has_norm:
        xn_ref = refs[pos]

        @pl.when(j == 0)
        def _():
            _rms_rows(x_ref, g_ref, xn_ref)

        lhs = xn_ref[...]
    else:
        lhs = x_ref[...]
    acc = jnp.dot(lhs, w_ref[...], preferred_element_type=_F32)
    epilogue(acc, i, j, extra_refs, out_refs)


def _fused_matmul(x, w, *, bm, bn, n_col_blocks, w_col_block_off=0, gain=None, extras=(),
                  extra_specs=(), out_shapes, out_specs, epilogue, name):
    m, k = x.shape
    assert m % bm == 0 and w.shape[0] == k and w.shape[1] % bn == 0
    has_norm = gain is not None
    in_specs = [pl.BlockSpec((bm, k), lambda i, j: (i, 0))]
    args = [x]
    if has_norm:
        in_specs.append(pl.BlockSpec((1, k), lambda i, j: (0, 0)))
        args.append(gain.reshape(1, k).astype(_F32))
    in_specs.append(pl.BlockSpec((k, bn), lambda i, j: (0, j + w_col_block_off)))
    args.append(w)
    in_specs.extend(extra_specs)
    args.extend(extras)
    scratch = [pltpu.VMEM((bm, k), _BF16)] if has_norm else []
    block_bytes = _nbytes((bm, k), x.dtype) + _nbytes((k, bn), w.dtype)
    for spec, arr in zip(extra_specs, extras):
        block_bytes += _nbytes([d for d in spec.block_shape if d is not None], arr.dtype)
    for spec, sh in zip(out_specs, out_shapes):
        block_bytes += _nbytes([d for d in spec.block_shape if d is not None], sh.dtype)
    scratch_bytes = _nbytes((bm, k), _BF16) if has_norm else 0
    body = functools.partial(_mm_body, has_norm=has_norm, n_extra=len(extras),
                             n_out=len(out_shapes), epilogue=epilogue)
    return pl.pallas_call(
        body,
        grid=(m // bm, n_col_blocks),
        in_specs=in_specs,
        out_specs=list(out_specs),
        out_shape=list(out_shapes),
        scratch_shapes=scratch,
        compiler_params=pltpu.CompilerParams(
            dimension_semantics=("parallel", "arbitrary"),
            vmem_limit_bytes=_vmem_limit(block_bytes, scratch_bytes, 3 * bm * bn * 4)),
        name=name,
    )(*args)


def _retention_qk(h, gain, w_in, cos, sin, seq):
    t, d = h.shape
    bm, bn = 1024, 1024
    dk = d // _RET_HEADS
    half = dk // 2
    n_q_blocks = d // bn
    k_scale = dk ** -0.5

    def epilogue(acc, i, j, ex, outs):
        o_ref = outs[0]
        cos_b = ex[0][...]
        sin_b = ex[1][...]
        scale = jnp.where(j < n_q_blocks, 1.0, k_scale).astype(_F32)
        for c in range(bn // dk):
            x1 = acc[:, c * dk:c * dk + half]
            x2 = acc[:, c * dk + half:(c + 1) * dk]
            o_ref[:, c * dk:c * dk + half] = ((x1 * cos_b - x2 * sin_b) * scale).astype(_BF16)
            o_ref[:, c * dk + half:(c + 1) * dk] = ((x1 * sin_b + x2 * cos_b) * scale).astype(_BF16)

    pos_blocks = seq // bm
    tab_spec = pl.BlockSpec((bm, half), lambda i, j: (i % pos_blocks, 0))
    return _fused_matmul(
        h, w_in, bm=bm, bn=bn, n_col_blocks=(2 * d) // bn, gain=gain,
        extras=(cos, sin), extra_specs=(tab_spec, tab_spec),
        out_shapes=[jax.ShapeDtypeStruct((t, 2 * d), _BF16)],
        out_specs=[pl.BlockSpec((bm, bn), lambda i, j: (i, j))],
        epilogue=epilogue, name="ret_qk")[0]


def _retention_vg(h, gain, w_in):
    t, d = h.shape
    bm, bn = 1024, 1024
    n_vg = w_in.shape[1] - 2 * d

    def epilogue(acc, i, j, ex, outs):
        outs[0][...] = acc.astype(_BF16)

    return _fused_matmul(
        h, w_in, bm=bm, bn=bn, n_col_blocks=n_vg // bn, w_col_block_off=(2 * d) // bn, gain=gain,
        out_shapes=[jax.ShapeDtypeStruct((t, n_vg), _BF16)],
        out_specs=[pl.BlockSpec((bm, bn), lambda i, j: (i, j))],
        epilogue=epilogue, name="ret_vg")[0]


def _matmul_residual_body(x_ref, w_ref, res_ref, o_ref):
    o_ref[...] = res_ref[...] + jnp.dot(x_ref[...], w_ref[...], preferred_element_type=_F32)


def _matmul_residual(x, w, res, *, bm, bn, name):
    t, k = x.shape
    n = w.shape[1]
    assert t % bm == 0 and n % bn == 0
    tile = pl.BlockSpec((bm, bn), lambda j, i: (i, j))
    block_bytes = (_nbytes((bm, k), x.dtype) + _nbytes((k, bn), w.dtype)
                   + 2 * _nbytes((bm, bn), _F32))
    return pl.pallas_call(
        _matmul_residual_body,
        grid=(n // bn, t // bm),
        in_specs=[pl.BlockSpec((bm, k), lambda j, i: (i, 0)),
                  pl.BlockSpec((k, bn), lambda j, i: (0, j)), tile],
        out_specs=tile,
        out_shape=jax.ShapeDtypeStruct((t, n), _F32),
        compiler_params=pltpu.CompilerParams(
            dimension_semantics=("parallel", "parallel"),
            vmem_limit_bytes=_vmem_limit(block_bytes, 0, 2 * _nbytes((bm, bn), _F32))),
        name=name,
    )(x, w, res)


def _norm_matmul(h, gain, w, *, bm, bn, act, name):
    t = h.shape[0]
    n = w.shape[1]

    def epilogue(acc, i, j, ex, outs):
        outs[0][...] = act(acc).astype(_BF16)

    return _fused_matmul(
        h, w, bm=bm, bn=bn, n_col_blocks=n // bn, gain=gain,
        out_shapes=[jax.ShapeDtypeStruct((t, n), _BF16)],
        out_specs=[pl.BlockSpec((bm, bn), lambda i, j: (i, j))],
        epilogue=epilogue, name=name)[0]


def _identity(a):
    return a


def _rms_rows_inplace(o_ref, g_ref):
    g = g_ref[...]

    def chunk(c, carry):
        rows = slice(c * _NORM_ROWS, (c + 1) * _NORM_ROWS)
        x = o_ref[rows, :]
        ms = jnp.mean(x * x, axis=-1, keepdims=True)
        o_ref[rows, :] = (x * lax.rsqrt(ms + _RMS_EPS)) * g
        return carry

    for c in range(o_ref.shape[0] // _NORM_ROWS):
        chunk(c, 0)


def _mlp_body(x_ref, g_ref, wu_ref, wd_ref, o_ref, xn_ref):
    f = pl.program_id(1)

    @pl.when(f == 0)
    def _():
        _rms_rows(x_ref, g_ref, xn_ref)
        o_ref[...] = x_ref[...]

    u = jnp.dot(xn_ref[...], wu_ref[...], preferred_element_type=_F32)
    r = jnp.maximum(u, 0.0)
    o_ref[...] += jnp.dot((r * r).astype(_BF16), wd_ref[...], preferred_element_type=_F32)


def _mlp(h, gain, w_up, w_down, name):
    t, d = h.shape
    ff = w_up.shape[1]
    bm, bf = 512, 1024
    block_bytes = 2 * _nbytes((bm, d), _F32) + _nbytes((d, bf), _BF16) + _nbytes((bf, d), _BF16)
    temp_bytes = _nbytes((bm, bf), _F32) * 2 + _nbytes((bm, d), _F32)
    return pl.pallas_call(
        _mlp_body,
        grid=(t // bm, ff // bf),
        in_specs=[
            pl.BlockSpec((bm, d), lambda i, f: (i, 0)),
            pl.BlockSpec((1, d), lambda i, f: (0, 0)),
            pl.BlockSpec((d, bf), lambda i, f: (0, f)),
            pl.BlockSpec((bf, d), lambda i, f: (f, 0)),
        ],
        out_specs=pl.BlockSpec((bm, d), lambda i, f: (i, 0)),
        out_shape=jax.ShapeDtypeStruct((t, d), _F32),
        scratch_shapes=[pltpu.VMEM((bm, d), _BF16)],
        compiler_params=pltpu.CompilerParams(
            dimension_semantics=("parallel", "arbitrary"),
            vmem_limit_bytes=_vmem_limit(block_bytes, _nbytes((bm, d), _BF16), temp_bytes)),
        name=name,
    )(h, gain.reshape(1, d).astype(_F32), w_up, w_down)


def _ple_body(*refs, final):
    if final:
        x_ref, g_ref, wg_ref, p_ref, wu_ref, gf_ref, o_ref, xn_ref = refs
    else:
        x_ref, g_ref, wg_ref, p_ref, wu_ref, o_ref, xn_ref = refs
    _rms_rows(x_ref, g_ref, xn_ref)
    gate = jnp.dot(xn_ref[...], wg_ref[...], preferred_element_type=_F32)
    up = jnp.dot(p_ref[...].astype(_BF16), wu_ref[...], preferred_element_type=_F32)
    o_ref[...] = x_ref[...] + up * jax.nn.sigmoid(gate)
    if final:
        _rms_rows_inplace(o_ref, gf_ref)


def _per_layer_embed(h, p_i, gain, w_up, w_gate, name, final_gain=None):
    t, d = h.shape
    pd = p_i.shape[1]
    bm = 512
    final = final_gain is not None
    row_spec = pl.BlockSpec((bm, d), lambda i: (i, 0))
    vec_spec = pl.BlockSpec((1, d), lambda i: (0, 0))
    in_specs = [row_spec, vec_spec, pl.BlockSpec((d, d), lambda i: (0, 0)),
                pl.BlockSpec((bm, pd), lambda i: (i, 0)), pl.BlockSpec((pd, d), lambda i: (0, 0))]
    args = [h, gain.reshape(1, d).astype(_F32), w_gate, p_i, w_up]
    if final:
        in_specs.append(vec_spec)
        args.append(final_gain.reshape(1, d).astype(_F32))
    block_bytes = (2 * _nbytes((bm, d), _F32) + _nbytes((d, d), _BF16) + _nbytes((bm, pd), _F32)
                   + _nbytes((pd, d), _BF16))
    return pl.pallas_call(
        functools.partial(_ple_body, final=final),
        grid=(t // bm,),
        in_specs=in_specs,
        out_specs=row_spec,
        out_shape=jax.ShapeDtypeStruct((t, d), _F32),
        scratch_shapes=[pltpu.VMEM((bm, d), _BF16)],
        compiler_params=pltpu.CompilerParams(
            dimension_semantics=("parallel",),
            vmem_limit_bytes=_vmem_limit(block_bytes, _nbytes((bm, d), _BF16),
                                         3 * _nbytes((bm, d), _F32))),
        name=name,
    )(*args)


def _shared_kv(h, gain, w_kv):
    t, d = h.shape
    n = w_kv.shape[1]
    bm, bn = 1024, 1024
    per = bm // _MOBA_BLOCK

    def epilogue(acc, i, j, ex, outs):
        outs[0][...] = acc.astype(_BF16)
        for r in range(per):
            blk = acc[r * _MOBA_BLOCK:(r + 1) * _MOBA_BLOCK, :]
            outs[1][0, r:r + 1, :] = jnp.mean(blk, axis=0, keepdims=True)

    kv, means = _fused_matmul(
        h, w_kv, bm=bm, bn=bn, n_col_blocks=n // bn, gain=gain,
        out_shapes=[jax.ShapeDtypeStruct((t, n), _BF16),
                    jax.ShapeDtypeStruct((t // bm, per, n), _F32)],
        out_specs=[pl.BlockSpec((bm, bn), lambda i, j: (i, j)),
                   pl.BlockSpec((1, per, bn), lambda i, j: (i, 0, j))],
        epilogue=epilogue, name="shared_kv")
    return kv, means.reshape(t // _MOBA_BLOCK, n)


def _ret_body(lg_ref, q_ref, k_ref, v_ref, g_ref, o_ref, state_ref, *, chunk, n_chunks):
    hd = pl.program_id(1)

    @pl.when(pl.program_id(2) == 0)
    def _():
        state_ref[...] = jnp.zeros_like(state_ref)

    lg = lg_ref[hd]
    ri = lax.broadcasted_iota(jnp.int32, (chunk, chunk), 0)
    ci = lax.broadcasted_iota(jnp.int32, (chunk, chunk), 1)
    diff = ri - ci
    decay = jnp.where(diff >= 0, jnp.exp(jnp.maximum(diff, 0).astype(_F32) * lg), 0.0)
    r1 = lax.broadcasted_iota(jnp.int32, (chunk, 1), 0)
    xi = jnp.exp((r1 + 1).astype(_F32) * lg)
    zeta = jnp.exp((chunk - 1 - r1).astype(_F32) * lg)
    chunk_decay = jnp.exp(jnp.full((1, 1), chunk, _F32) * lg)

    def step(c, carry):
        r = pl.multiple_of(c * chunk, chunk)
        qc = q_ref[pl.ds(r, chunk), :]
        kc = k_ref[pl.ds(r, chunk), :]
        vc = v_ref[pl.ds(r, chunk), :]
        scores = lax.dot_general(qc, kc, (((1,), (1,)), ((), ())),
                                 preferred_element_type=_F32) * decay
        intra = jnp.dot(scores.astype(_BF16), vc, preferred_element_type=_F32)
        state = state_ref[...]
        inter = jnp.dot(qc, state.astype(_BF16), preferred_element_type=_F32) * xi
        kz_t = jnp.transpose(kc.astype(_F32) * zeta).astype(_BF16)
        state_ref[...] = state * chunk_decay + jnp.dot(kz_t, vc, preferred_element_type=_F32)
        y = intra + inter
        mu = jnp.mean(y, axis=-1, keepdims=True)
        yc = y - mu
        var = jnp.mean(yc * yc, axis=-1, keepdims=True)
        yn = yc * lax.rsqrt(var + _GN_EPS)
        gate = g_ref[pl.ds(r, chunk), :].astype(_F32)
        o_ref[pl.ds(r, chunk), :] = (gate * jax.nn.sigmoid(gate) * yn).astype(_BF16)
        return carry

    lax.fori_loop(0, n_chunks, step, 0, unroll=True)


def _retention_core(qk, vg, batch, seq, d):
    t = qk.shape[0]
    dk = d // _RET_HEADS
    dv = 2 * dk
    vdim = _RET_HEADS * dv
    tb = 1024
    nt = seq // tb
    log_gamma = jnp.log1p(-jnp.exp2(-5.0 - jnp.arange(_RET_HEADS, dtype=_F32)))

    def row(b, h, s):
        return b * nt + s

    block_bytes = 2 * _nbytes((tb, dk), _BF16) + 3 * _nbytes((tb, dv), _BF16)
    return pl.pallas_call(
        functools.partial(_ret_body, chunk=_RET_CHUNK, n_chunks=tb // _RET_CHUNK),
        grid=(batch, _RET_HEADS, nt),
        in_specs=[
            pl.BlockSpec(memory_space=pltpu.SMEM),
            pl.BlockSpec((tb, dk), lambda b, h, s: (row(b, h, s), h)),
            pl.BlockSpec((tb, dk), lambda b, h, s: (row(b, h, s), _RET_HEADS + h)),
            pl.BlockSpec((tb, dv), lambda b, h, s: (row(b, h, s), h)),
            pl.BlockSpec((tb, dv), lambda b, h, s: (row(b, h, s), _RET_HEADS + h)),
        ],
        out_specs=pl.BlockSpec((tb, dv), lambda b, h, s: (row(b, h, s), h)),
        out_shape=jax.ShapeDtypeStruct((t, vdim), _BF16),
        scratch_shapes=[pltpu.VMEM((dk, dv), _F32)],
        compiler_params=pltpu.CompilerParams(
            dimension_semantics=("parallel", "parallel", "arbitrary"),
            vmem_limit_bytes=_vmem_limit(block_bytes, _nbytes((dk, dv), _F32), 8 << 20)),
        name="retention_core",
    )(log_gamma, qk, qk, vg, vg)


def _t5_thresholds():
    max_exact = _REL_BUCKETS // 2
    n_log = _REL_BUCKETS - max_exact
    ratio = _REL_MAX_DIST / max_exact
    thr = list(range(max_exact + 1))
    for kk in range(1, n_log):
        guess = int(math.floor(max_exact * ratio ** (kk / n_log)))
        n = max(guess - 2, max_exact)
        while not (n / max_exact) ** n_log >= ratio ** kk:
            n += 1
        thr.append(n)
    return thr


def _moba_body(rb_ref, q_ref, k_ref, v_ref, km_ref, o_ref, bias_ref, vt_ref, *,
               blk, nblk, n_tiles, thresholds, scale, topk, group):
    hd = pl.program_id(0)
    gstep = pl.program_id(2)

    @pl.when((pl.program_id(1) == 0) & (gstep == 0))
    def _():
        ki = lax.broadcasted_iota(jnp.int32, (blk, blk), 0)
        qi = lax.broadcasted_iota(jnp.int32, (blk, blk), 1)

        def build(dd, carry):
            dist = qi - ki + dd * blk
            val = jnp.full((blk, blk), rb_ref[0, hd], _F32)
            for b in range(1, _REL_BUCKETS):
                val = jnp.where(dist >= thresholds[b], rb_ref[b, hd], val)
            bias_ref[dd] = jnp.where(dist >= 0, val * _LOG2E, _NEG_INF)
            return carry

        lax.fori_loop(0, n_tiles, build, 0)
        bias_ref[n_tiles] = jnp.full((blk, blk), rb_ref[_REL_BUCKETS - 1, hd] * _LOG2E, _F32)

    sub = 8

    @pl.when(gstep == 0)
    def _():
        dh = v_ref.shape[1]
        vt_ref[dh:dh + 16, :] = jnp.ones((16, vt_ref.shape[1]), _BF16)
        for n in range(nblk):
            v_blk = v_ref[n * blk:(n + 1) * blk, :].astype(_F32)
            vt_ref[0:dh, n * blk:(n + 1) * blk] = jnp.transpose(v_blk).astype(_BF16)

    km = km_ref[...].astype(_BF16)
    row = lax.broadcasted_iota(jnp.int32, (nblk, blk), 0)

    def attend_query_block(qb, r):
        q = q_ref[r * blk:(r + 1) * blk, :]
        gate = lax.dot_general(km, q, (((1,), (1,)), ((), ())), preferred_element_type=_F32)
        past = row < qb
        g = jnp.where(past, gate, _NEG_INF)
        sel = jnp.zeros((nblk, blk), jnp.bool_)
        for _ in range(topk):
            mx = jnp.max(g, axis=0, keepdims=True)
            idx = jnp.min(jnp.where(g == mx, row, nblk), axis=0, keepdims=True)
            pick = row == idx
            sel = jnp.logical_or(sel, pick)
            g = jnp.where(pick, -jnp.inf, g)
        attend = jnp.logical_or(jnp.logical_and(sel, past), row == qb)
        pen = jnp.where(attend, 0.0, _NEG_INF).astype(_F32)
        s = []
        m_part = None
        for n in range(qb + 1):
            kq = lax.dot_general(k_ref[n * blk:(n + 1) * blk, :], q, (((1,), (1,)), ((), ())),
                                 preferred_element_type=_F32)
            sn = kq * (scale * _LOG2E) + bias_ref[min(qb - n, n_tiles)]
            s.append(sn)
            part = jnp.max(sn.reshape(blk // sub, sub, blk), axis=0) + pen[n:n + 1, :]
            m_part = part if m_part is None else jnp.maximum(m_part, part)
        m = jnp.max(m_part, axis=0, keepdims=True)
        acc = None
        for n in range(qb + 1):
            p = jnp.exp2(s[n] - (m - pen[n:n + 1, :]))
            pv = jnp.dot(vt_ref[:, n * blk:(n + 1) * blk], p.astype(_BF16),
                         preferred_element_type=_F32)
            acc = pv if acc is None else acc + pv
        dh = q.shape[1]
        l = acc[dh:dh + 1, :]
        o_ref[r * blk:(r + 1) * blk, :] = jnp.transpose(acc[0:dh] / l).astype(_BF16)

    for gi in range(nblk // group):
        @pl.when(gstep == gi)
        def _(gi=gi):
            for r in range(group):
                attend_query_block(gi * group + r, r)


_MOBA_GROUP = 16


def _moba_attention(q, kv, k_means, rel_bias, batch, seq):
    t, d = q.shape
    dh = d // _ATT_HEADS
    blk = _MOBA_BLOCK
    nblk = seq // blk
    thresholds = _t5_thresholds()
    n_tiles = 0
    while n_tiles * blk - (blk - 1) < thresholds[_REL_BUCKETS - 1]:
        n_tiles += 1
    n_tiles = min(n_tiles, nblk)
    group = math.gcd(_MOBA_GROUP, nblk)
    steps = nblk // group
    block_bytes = (2 * _nbytes((group * blk, dh), _BF16) + 2 * _nbytes((seq, dh), _BF16)
                   + _nbytes((nblk, dh), _F32))
    scratch_bytes = (n_tiles + 1) * blk * blk * 4 + _nbytes((dh + 16, seq), _BF16)
    return pl.pallas_call(
        functools.partial(_moba_body, blk=blk, nblk=nblk, n_tiles=n_tiles, thresholds=thresholds,
                          scale=dh ** -0.5, topk=min(_MOBA_TOPK, nblk), group=group),
        grid=(_ATT_HEADS, batch, steps),
        in_specs=[
            pl.BlockSpec(memory_space=pltpu.SMEM),
            pl.BlockSpec((group * blk, dh), lambda h, b, s: (b * steps + s, h)),
            pl.BlockSpec((seq, dh), lambda h, b, s: (b, h)),
            pl.BlockSpec((seq, dh), lambda h, b, s: (b, _ATT_HEADS + h)),
            pl.BlockSpec((nblk, dh), lambda h, b, s: (b, h)),
        ],
        out_specs=pl.BlockSpec((group * blk, dh), lambda h, b, s: (b * steps + s, h)),
        out_shape=jax.ShapeDtypeStruct((t, d), _BF16),
        scratch_shapes=[
            pltpu.VMEM((n_tiles + 1, blk, blk), _F32),
            pltpu.VMEM((dh + 16, seq), _BF16),
        ],
        compiler_params=pltpu.CompilerParams(
            dimension_semantics=("arbitrary", "arbitrary", "arbitrary"),
            vmem_limit_bytes=_vmem_limit(block_bytes, scratch_bytes, 16 << 20)),
        name="moba_attention",
    )(rel_bias.astype(_F32), q, kv, kv, k_means)


def _rotary_tables(seq, half):
    inv = _ROPE_BASE ** (-jnp.arange(half, dtype=_F32) / half)
    ang = jnp.arange(seq).astype(_F32)[:, None] * inv[None, :]
    return jnp.cos(ang), jnp.sin(ang)


def kernel(x, p, ret_norm_g, ret_w_in, ret_w_out, kv_norm_g, w_kv, att_norm_g, att_w_q, att_w_o,
           rel_bias, mlp_norm_g, mlp_w_up, mlp_w_down, ple_norm_g, ple_w_up, ple_w_gate,
           final_norm_g):
    batch, seq, d = x.shape
    depth = p.shape[0]
    n_ret = ret_w_in.shape[0]
    t = batch * seq
    h = x.reshape(t, d)
    p2 = p.reshape(depth, t, p.shape[-1])
    cos, sin = _rotary_tables(seq, (d // _RET_HEADS) // 2)
    kv = k_means = None
    for i in range(depth):
        if i < n_ret:
            w_in = ret_w_in[i].astype(_BF16)
            qk = _retention_qk(h, ret_norm_g[i], w_in, cos, sin, seq)
            vg = _retention_vg(h, ret_norm_g[i], w_in)
            y = _retention_core(qk, vg, batch, seq, d)
            h = _matmul_residual(y, ret_w_out[i].astype(_BF16), h, bm=1024, bn=1024, name="ret_out")
        else:
            j = i - n_ret
            q = _norm_matmul(h, att_norm_g[j], att_w_q[j].astype(_BF16), bm=1024, bn=1024,
                             act=_identity, name="att_q")
            o = _moba_attention(q, kv, k_means, rel_bias, batch, seq)
            h = _matmul_residual(o, att_w_o[j].astype(_BF16), h, bm=1024, bn=1024, name="att_out")
        h = _mlp(h, mlp_norm_g[i], mlp_w_up[i].astype(_BF16), mlp_w_down[i].astype(_BF16),
                 name=f"mlp{i}")
        h = _per_layer_embed(h, p2[i], ple_norm_g[i], ple_w_up[i].astype(_BF16),
                             ple_w_gate[i].astype(_BF16), name=f"ple{i}",
                             final_gain=final_norm_g if i == depth - 1 else None)
        if i == n_ret - 1:
            kv, k_means = _shared_kv(h, kv_norm_g, w_kv.astype(_BF16))
    return h.reshape(batch, seq, d)
```

```python
import functools
import math

import jax
import jax.numpy as jnp
from jax import lax
from jax.experimental import pallas as pl
from jax.experimental.pallas import tpu as pltpu

_F32 = jnp.float32
_BF16 = jnp.bfloat16

_RET_HEADS = 8
_RET_CHUNK = 128
_ROPE_BASE = 10000.0
_GN_EPS = 1e-5
_ATT_HEADS = 16
_MOBA_BLOCK = 256
_MOBA_TOPK = 3
_REL_BUCKETS = 32
_REL_MAX_DIST = 1024
_RMS_EPS = 1e-6
_NEG_INF = -1e30
_LOG2E = math.log2(math.e)

_V7X_VMEM_BUDGET = 60 * 1024 * 1024
_NORM_ROWS = 32
_NORM_UNROLL = 8


def _vmem_limit(block_bytes, scratch_bytes=0, temp_bytes=0):
    need = 2 * block_bytes + scratch_bytes + temp_bytes + (4 << 20)
    return int(min(max(need, 16 << 20), _V7X_VMEM_BUDGET))


def _nbytes(shape, dtype):
    return math.prod(shape) * jnp.dtype(dtype).itemsize


def _rms_rows(x_ref, g_ref, xn_ref):
    g = g_ref[...]

    def chunk(c, carry):
        r = pl.multiple_of(c * _NORM_ROWS, _NORM_ROWS)
        x = x_ref[pl.ds(r, _NORM_ROWS), :].astype(_F32)
        ms = jnp.mean(x * x, axis=-1, keepdims=True)
        xn_ref[pl.ds(r, _NORM_ROWS), :] = ((x * lax.rsqrt(ms + _RMS_EPS)) * g).astype(_BF16)
        return carry

    lax.fori_loop(0, x_ref.shape[0] // _NORM_ROWS, chunk, 0, unroll=_NORM_UNROLL)


def _mm_body(*refs, has_norm, n_extra, n_out, epilogue):
    x_ref = refs[0]
    pos = 1
    g_ref = None
    if has_norm:
        g_ref = refs[pos]
        pos += 1
    w_ref = refs[pos]
    pos += 1
    extra_refs = refs[pos:pos + n_extra]
    pos += n_extra
    out_refs = refs[pos:pos + n_out]
    pos += n_out
    i = pl.program_id(0)
    j = pl.program_id(1)
    if has_norm:
        xn_ref = refs[pos]

        @pl.when(j == 0)
        def _():
            _rms_rows(x_ref, g_ref, xn_ref)

        lhs = xn_ref[...]
    else:
        lhs = x_ref[...]
    acc = jnp.dot(lhs, w_ref[...], preferred_element_type=_F32)
    epilogue(acc, i, j, extra_refs, out_refs)


def _fused_matmul(x, w, *, bm, bn, n_col_blocks, w_col_block_off=0, gain=None, extras=(),
                  extra_specs=(), out_shapes, out_specs, epilogue, name):
    m, k = x.shape
    assert m % bm == 0 and w.shape[0] == k and w.shape[1] % bn == 0
    has_norm = gain is not None
    in_specs = [pl.BlockSpec((bm, k), lambda i, j: (i, 0))]
    args = [x]
    if has_norm:
        in_specs.append(pl.BlockSpec((1, k), lambda i, j: (0, 0)))
        args.append(gain.reshape(1, k).astype(_F32))
    in_specs.append(pl.BlockSpec((k, bn), lambda i, j: (0, j + w_col_block_off)))
    args.append(w)
    in_specs.extend(extra_specs)
    args.extend(extras)
    scratch = [pltpu.VMEM((bm, k), _BF16)] if has_norm else []
    block_bytes = _nbytes((bm, k), x.dtype) + _nbytes((k, bn), w.dtype)
    for spec, arr in zip(extra_specs, extras):
        block_bytes += _nbytes([d for d in spec.block_shape if d is not None], arr.dtype)
    for spec, sh in zip(out_specs, out_shapes):
        block_bytes += _nbytes([d for d in spec.block_shape if d is not None], sh.dtype)
    scratch_bytes = _nbytes((bm, k), _BF16) if has_norm else 0
    body = functools.partial(_mm_body, has_norm=has_norm, n_extra=len(extras),
                             n_out=len(out_shapes), epilogue=epilogue)
    return pl.pallas_call(
        body,
        grid=(m // bm, n_col_blocks),
        in_specs=in_specs,
        out_specs=list(out_specs),
        out_shape=list(out_shapes),
        scratch_shapes=scratch,
        compiler_params=pltpu.CompilerParams(
            dimension_semantics=("parallel", "arbitrary"),
            vmem_limit_bytes=_vmem_limit(block_bytes, scratch_bytes, 3 * bm * bn * 4)),
        name=name,
    )(*args)


def _retention_qk(h, gain, w_in, cos, sin, seq):
    t, d = h.shape
    bm, bn = 1024, 1024
    dk = d // _RET_HEADS
    half = dk // 2
    n_q_blocks = d // bn
    k_scale = dk ** -0.5

    def epilogue(acc, i, j, ex, outs):
        o_ref = outs[0]
        cos_b = ex[0][...]
        sin_b = ex[1][...]
        scale = jnp.where(j < n_q_blocks, 1.0, k_scale).astype(_F32)
        for c in range(bn // dk):
            x1 = acc[:, c * dk:c * dk + half]
            x2 = acc[:, c * dk + half:(c + 1) * dk]
            o_ref[:, c * dk:c * dk + half] = ((x1 * cos_b - x2 * sin_b) * scale).astype(_BF16)
            o_ref[:, c * dk + half:(c + 1) * dk] = ((x1 * sin_b + x2 * cos_b) * scale).astype(_BF16)

    pos_blocks = seq // bm
    tab_spec = pl.BlockSpec((bm, half), lambda i, j: (i % pos_blocks, 0))
    return _fused_matmul(
        h, w_in, bm=bm, bn=bn, n_col_blocks=(2 * d) // bn, gain=gain,
        extras=(cos, sin), extra_specs=(tab_spec, tab_spec),
        out_shapes=[jax.ShapeDtypeStruct((t, 2 * d), _BF16)],
        out_specs=[pl.BlockSpec((bm, bn), lambda i, j: (i, j))],
        epilogue=epilogue, name="ret_qk")[0]


def _retention_vg(h, gain, w_in):
    t, d = h.shape
    bm, bn = 1024, 1024
    n_vg = w_in.shape[1] - 2 * d

    def epilogue(acc, i, j, ex, outs):
        outs[0][...] = acc.astype(_BF16)

    return _fused_matmul(
        h, w_in, bm=bm, bn=bn, n_col_blocks=n_vg // bn, w_col_block_off=(2 * d) // bn, gain=gain,
        out_shapes=[jax.ShapeDtypeStruct((t, n_vg), _BF16)],
        out_specs=[pl.BlockSpec((bm, bn), lambda i, j: (i, j))],
        epilogue=epilogue, name="ret_vg")[0]


def _matmul_residual_body(x_ref, w_ref, res_ref, o_ref):
    o_ref[...] = res_ref[...] + jnp.dot(x_ref[...], w_ref[...], preferred_element_type=_F32)


def _matmul_residual(x, w, res, *, bm, bn, name):
    t, k = x.shape
    n = w.shape[1]
    assert t % bm == 0 and n % bn == 0
    tile = pl.BlockSpec((bm, bn), lambda j, i: (i, j))
    block_bytes = (_nbytes((bm, k), x.dtype) + _nbytes((k, bn), w.dtype)
                   + 2 * _nbytes((bm, bn), _F32))
    return pl.pallas_call(
        _matmul_residual_body,
        grid=(n // bn, t // bm),
        in_specs=[pl.BlockSpec((bm, k), lambda j, i: (i, 0)),
                  pl.BlockSpec((k, bn), lambda j, i: (0, j)), tile],
        out_specs=tile,
        out_shape=jax.ShapeDtypeStruct((t, n), _F32),
        compiler_params=pltpu.CompilerParams(
            dimension_semantics=("parallel", "parallel"),
            vmem_limit_bytes=_vmem_limit(block_bytes, 0, 2 * _nbytes((bm, bn), _F32))),
        name=name,
    )(x, w, res)


def _norm_matmul(h, gain, w, *, bm, bn, act, name):
    t = h.shape[0]
    n = w.shape[1]

    def epilogue(acc, i, j, ex, outs):
        outs[0][...] = act(acc).astype(_BF16)

    return _fused_matmul(
        h, w, bm=bm, bn=bn, n_col_blocks=n // bn, gain=gain,
        out_shapes=[jax.ShapeDtypeStruct((t, n), _BF16)],
        out_specs=[pl.BlockSpec((bm, bn), lambda i, j: (i, j))],
        epilogue=epilogue, name=name)[0]


def _identity(a):
    return a


def _rms_rows_inplace(o_ref, g_ref):
    g = g_ref[...]

    def chunk(c, carry):
        rows = slice(c * _NORM_ROWS, (c + 1) * _NORM_ROWS)
        x = o_ref[rows, :]
        ms = jnp.mean(x * x, axis=-1, keepdims=True)
        o_ref[rows, :] = (x * lax.rsqrt(ms + _RMS_EPS)) * g
        return carry

    for c in range(o_ref.shape[0] // _NORM_ROWS):
        chunk(c, 0)


def _mlp_body(x_ref, g_ref, wu_ref, wd_ref, o_ref, xn_ref):
    f = pl.program_id(1)

    @pl.when(f == 0)
    def _():
        _rms_rows(x_ref, g_ref, xn_ref)
        o_ref[...] = x_ref[...]

    u = jnp.dot(xn_ref[...], wu_ref[...], preferred_element_type=_F32)
    r = jnp.maximum(u, 0.0)
    o_ref[...] += jnp.dot((r * r).astype(_BF16), wd_ref[...], preferred_element_type=_F32)


def _mlp(h, gain, w_up, w_down, name):
    t, d = h.shape
    ff = w_up.shape[1]
    bm, bf = 512, 1024
    block_bytes = 2 * _nbytes((bm, d), _F32) + _nbytes((d, bf), _BF16) + _nbytes((bf, d), _BF16)
    temp_bytes = _nbytes((bm, bf), _F32) * 2 + _nbytes((bm, d), _F32)
    return pl.pallas_call(
        _mlp_body,
        grid=(t // bm, ff // bf),
        in_specs=[
            pl.BlockSpec((bm, d), lambda i, f: (i, 0)),
            pl.BlockSpec((1, d), lambda i, f: (0, 0)),
            pl.BlockSpec((d, bf), lambda i, f: (0, f)),
            pl.BlockSpec((bf, d), lambda i, f: (f, 0)),
        ],
        out_specs=pl.BlockSpec((bm, d), lambda i, f: (i, 0)),
        out_shape=jax.ShapeDtypeStruct((t, d), _F32),
        scratch_shapes=[pltpu.VMEM((bm, d), _BF16)],
        compiler_params=pltpu.CompilerParams(
            dimension_semantics=("parallel", "arbitrary"),
            vmem_limit_bytes=_vmem_limit(block_bytes, _nbytes((bm, d), _BF16), temp_bytes)),
        name=name,
    )(h, gain.reshape(1, d).astype(_F32), w_up, w_down)


def _ple_body(*refs, final):
    if final:
        x_ref, g_ref, wg_ref, p_ref, wu_ref, gf_ref, o_ref, xn_ref = refs
    else:
        x_ref, g_ref, wg_ref, p_ref, wu_ref, o_ref, xn_ref = refs
    _rms_rows(x_ref, g_ref, xn_ref)
    gate = jnp.dot(xn_ref[...], wg_ref[...], preferred_element_type=_F32)
    up = jnp.dot(p_ref[...].astype(_BF16), wu_ref[...], preferred_element_type=_F32)
    o_ref[...] = x_ref[...] + up * jax.nn.sigmoid(gate)
    if final:
        _rms_rows_inplace(o_ref, gf_ref)


def _per_layer_embed(h, p_i, gain, w_up, w_gate, name, final_gain=None):
    t, d = h.shape
    pd = p_i.shape[1]
    bm = 512
    final = final_gain is not None
    row_spec = pl.BlockSpec((bm, d), lambda i: (i, 0))
    vec_spec = pl.BlockSpec((1, d), lambda i: (0, 0))
    in_specs = [row_spec, vec_spec, pl.BlockSpec((d, d), lambda i: (0, 0)),
                pl.BlockSpec((bm, pd), lambda i: (i, 0)), pl.BlockSpec((pd, d), lambda i: (0, 0))]
    args = [h, gain.reshape(1, d).astype(_F32), w_gate, p_i, w_up]
    if final:
        in_specs.append(vec_spec)
        args.append(final_gain.reshape(1, d).astype(_F32))
    block_bytes = (2 * _nbytes((bm, d), _F32) + _nbytes((d, d), _BF16) + _nbytes((bm, pd), _F32)
                   + _nbytes((pd, d), _BF16))
    return pl.pallas_call(
        functools.partial(_ple_body, final=final),
        grid=(t // bm,),
        in_specs=in_specs,
        out_specs=row_spec,
        out_shape=jax.ShapeDtypeStruct((t, d), _F32),
        scratch_shapes=[pltpu.VMEM((bm, d), _BF16)],
        compiler_params=pltpu.CompilerParams(
            dimension_semantics=("parallel",),
            vmem_limit_bytes=_vmem_limit(block_bytes, _nbytes((bm, d), _BF16),
                                         3 * _nbytes((bm, d), _F32))),
        name=name,
    )(*args)


def _shared_kv(h, gain, w_kv):
    t, d = h.shape
    n = w_kv.shape[1]
    bm, bn = 1024, 1024
    per = bm // _MOBA_BLOCK

    def epilogue(acc, i, j, ex, outs):
        outs[0][...] = acc.astype(_BF16)
        for r in range(per):
            blk = acc[r * _MOBA_BLOCK:(r + 1) * _MOBA_BLOCK, :]
            outs[1][0, r:r + 1, :] = jnp.mean(blk, axis=0, keepdims=True)

    kv, means = _fused_matmul(
        h, w_kv, bm=bm, bn=bn, n_col_blocks=n // bn, gain=gain,
        out_shapes=[jax.ShapeDtypeStruct((t, n), _BF16),
                    jax.ShapeDtypeStruct((t // bm, per, n), _F32)],
        out_specs=[pl.BlockSpec((bm, bn), lambda i, j: (i, j)),
                   pl.BlockSpec((1, per, bn), lambda i, j: (i, 0, j))],
        epilogue=epilogue, name="shared_kv")
    return kv, means.reshape(t // _MOBA_BLOCK, n)


def _ret_body(lg_ref, q_ref, k_ref, v_ref, g_ref, o_ref, state_ref, *, chunk, n_chunks):
    hd = pl.program_id(1)

    @pl.when(pl.program_id(2) == 0)
    def _():
        state_ref[...] = jnp.zeros_like(state_ref)

    lg = lg_ref[hd]
    ri = lax.broadcasted_iota(jnp.int32, (chunk, chunk), 0)
    ci = lax.broadcasted_iota(jnp.int32, (chunk, chunk), 1)
    diff = ri - ci
    decay = jnp.where(diff >= 0, jnp.exp(jnp.maximum(diff, 0).astype(_F32) * lg), 0.0)
    r1 = lax.broadcasted_iota(jnp.int32, (chunk, 1), 0)
    xi = jnp.exp((r1 + 1).astype(_F32) * lg)
    zeta = jnp.exp((chunk - 1 - r1).astype(_F32) * lg)
    chunk_decay = jnp.exp(jnp.full((1, 1), chunk, _F32) * lg)

    def step(c, carry):
        r = pl.multiple_of(c * chunk, chunk)
        qc = q_ref[pl.ds(r, chunk), :]
        kc = k_ref[pl.ds(r, chunk), :]
        vc = v_ref[pl.ds(r, chunk), :]
        scores = lax.dot_general(qc, kc, (((1,), (1,)), ((), ())),
                                 preferred_element_type=_F32) * decay
        intra = jnp.dot(scores.astype(_BF16), vc, preferred_element_type=_F32)
        state = state_ref[...]
        inter = jnp.dot(qc, state.astype(_BF16), preferred_element_type=_F32) * xi
        kz_t = jnp.transpose(kc.astype(_F32) * zeta).astype(_BF16)
        state_ref[...] = state * chunk_decay + jnp.dot(kz_t, vc, preferred_element_type=_F32)
        y = intra + inter
        mu = jnp.mean(y, axis=-1, keepdims=True)
        yc = y - mu
        var = jnp.mean(yc * yc, axis=-1, keepdims=True)
        yn = yc * lax.rsqrt(var + _GN_EPS)
        gate = g_ref[pl.ds(r, chunk), :].astype(_F32)
        o_ref[pl.ds(r, chunk), :] = (gate * jax.nn.sigmoid(gate) * yn).astype(_BF16)
        return carry

    lax.fori_loop(0, n_chunks, step, 0, unroll=True)


def _retention_core(qk, vg, batch, seq, d):
    t = qk.shape[0]
    dk = d // _RET_HEADS
    dv = 2 * dk
    vdim = _RET_HEADS * dv
    tb = 1024
    nt = seq // tb
    log_gamma = jnp.log1p(-jnp.exp2(-5.0 - jnp.arange(_RET_HEADS, dtype=_F32)))

    def row(b, h, s):
        return b * nt + s

    block_bytes = 2 * _nbytes((tb, dk), _BF16) + 3 * _nbytes((tb, dv), _BF16)
    return pl.pallas_call(
        functools.partial(_ret_body, chunk=_RET_CHUNK, n_chunks=tb // _RET_CHUNK),
        grid=(batch, _RET_HEADS, nt),
        in_specs=[
            pl.BlockSpec(memory_space=pltpu.SMEM),
            pl.BlockSpec((tb, dk), lambda b, h, s: (row(b, h, s), h)),
            pl.BlockSpec((tb, dk), lambda b, h, s: (row(b, h, s), _RET_HEADS + h)),
            pl.BlockSpec((tb, dv), lambda b, h, s: (row(b, h, s), h)),
            pl.BlockSpec((tb, dv), lambda b, h, s: (row(b, h, s), _RET_HEADS + h)),
        ],
        out_specs=pl.BlockSpec((tb, dv), lambda b, h, s: (row(b, h, s), h)),
        out_shape=jax.ShapeDtypeStruct((t, vdim), _BF16),
        scratch_shapes=[pltpu.VMEM((dk, dv), _F32)],
        compiler_params=pltpu.CompilerParams(
            dimension_semantics=("parallel", "parallel", "arbitrary"),
            vmem_limit_bytes=_vmem_limit(block_bytes, _nbytes((dk, dv), _F32), 8 << 20)),
        name="retention_core",
    )(log_gamma, qk, qk, vg, vg)


def _t5_thresholds():
    max_exact = _REL_BUCKETS // 2
    n_log = _REL_BUCKETS - max_exact
    ratio = _REL_MAX_DIST / max_exact
    thr = list(range(max_exact + 1))
    for kk in range(1, n_log):
        guess = int(math.floor(max_exact * ratio ** (kk / n_log)))
        n = max(guess - 2, max_exact)
        while not (n / max_exact) ** n_log >= ratio ** kk:
            n += 1
        thr.append(n)
    return thr


def _moba_body(rb_ref, q_ref, k_ref, v_ref, km_ref, o_ref, bias_ref, vt_ref, *,
               blk, nblk, n_tiles, thresholds, scale, topk, group):
    hd = pl.program_id(0)
    gstep = pl.program_id(2)

    @pl.when((pl.program_id(1) == 0) & (gstep == 0))
    def _():
        ki = lax.broadcasted_iota(jnp.int32, (blk, blk), 0)
        qi = lax.broadcasted_iota(jnp.int32, (blk, blk), 1)

        def build(dd, carry):
            dist = qi - ki + dd * blk
            val = jnp.full((blk, blk), rb_ref[0, hd], _F32)
            for b in range(1, _REL_BUCKETS):
                val = jnp.where(dist >= thresholds[b], rb_ref[b, hd], val)
            bias_ref[dd] = jnp.where(dist >= 0, val * _LOG2E, _NEG_INF)
            return carry

        lax.fori_loop(0, n_tiles, build, 0)
        bias_ref[n_tiles] = jnp.full((blk, blk), rb_ref[_REL_BUCKETS - 1, hd] * _LOG2E, _F32)

    sub = 8

    @pl.when(gstep == 0)
    def _():
        dh = v_ref.shape[1]
        vt_ref[dh:dh + 16, :] = jnp.ones((16, vt_ref.shape[1]), _BF16)
        for n in range(nblk):
            v_blk = v_ref[n * blk:(n + 1) * blk, :].astype(_F32)
            vt_ref[0:dh, n * blk:(n + 1) * blk] = jnp.transpose(v_blk).astype(_BF16)

    km = km_ref[...].astype(_BF16)
    row = lax.broadcasted_iota(jnp.int32, (nblk, blk), 0)

    def attend_query_block(qb, r):
        q = q_ref[r * blk:(r + 1) * blk, :]
        gate = lax.dot_general(km, q, (((1,), (1,)), ((), ())), preferred_element_type=_F32)
        past = row < qb
        g = jnp.where(past, gate, _NEG_INF)
        sel = jnp.zeros((nblk, blk), jnp.bool_)
        for _ in range(topk):
            mx = jnp.max(g, axis=0, keepdims=True)
            idx = jnp.min(jnp.where(g == mx, row, nblk), axis=0, keepdims=True)
            pick = row == idx
            sel = jnp.logical_or(sel, pick)
            g = jnp.where(pick, -jnp.inf, g)
        attend = jnp.logical_or(jnp.logical_and(sel, past), row == qb)
        pen = jnp.where(attend, 0.0, _NEG_INF).astype(_F32)
        s = []
        m_part = None
        for n in range(qb + 1):
            kq = lax.dot_general(k_ref[n * blk:(n + 1) * blk, :], q, (((1,), (1,)), ((), ())),
                                 preferred_element_type=_F32)
            sn = kq * (scale * _LOG2E) + bias_ref[min(qb - n, n_tiles)]
            s.append(sn)
            part = jnp.max(sn.reshape(blk // sub, sub, blk), axis=0) + pen[n:n + 1, :]
            m_part = part if m_part is None else jnp.maximum(m_part, part)
        m = jnp.max(m_part, axis=0, keepdims=True)
        acc = None
        for n in range(qb + 1):
            p = jnp.exp2(s[n] - (m - pen[n:n + 1, :]))
            pv = jnp.dot(vt_ref[:, n * blk:(n + 1) * blk], p.astype(_BF16),
                         preferred_element_type=_F32)
            acc = pv if acc is None else acc + pv
        dh = q.shape[1]
        l = acc[dh:dh + 1, :]
        o_ref[r * blk:(r + 1) * blk, :] = jnp.transpose(acc[0:dh] / l).astype(_BF16)

    for gi in range(nblk // group):
        @pl.when(gstep == gi)
        def _(gi=gi):
            for r in range(group):
                attend_query_block(gi * group + r, r)


_MOBA_GROUP = 16


def _moba_attention(q, kv, k_means, rel_bias, batch, seq):
    t, d = q.shape
    dh = d // _ATT_HEADS
    blk = _MOBA_BLOCK
    nblk = seq // blk
    thresholds = _t5_thresholds()
    n_tiles = 0
    while n_tiles * blk - (blk - 1) < thresholds[_REL_BUCKETS - 1]:
        n_tiles += 1
    n_tiles = min(n_tiles, nblk)
    group = math.gcd(_MOBA_GROUP, nblk)
    steps = nblk // group
    block_bytes = (2 * _nbytes((group * blk, dh), _BF16) + 2 * _nbytes((seq, dh), _BF16)
                   + _nbytes((nblk, dh), _F32))
    scratch_bytes = (n_tiles + 1) * blk * blk * 4 + _nbytes((dh + 16, seq), _BF16)
    return pl.pallas_call(
        functools.partial(_moba_body, blk=blk, nblk=nblk, n_tiles=n_tiles, thresholds=thresholds,
                          scale=dh ** -0.5, topk=min(_MOBA_TOPK, nblk), group=group),
        grid=(_ATT_HEADS, batch, steps),
        in_specs=[
            pl.BlockSpec(memory_space=pltpu.SMEM),
            pl.BlockSpec((group * blk, dh), lambda h, b, s: (b * steps + s, h)),
            pl.BlockSpec((seq, dh), lambda h, b, s: (b, h)),
            pl.BlockSpec((seq, dh), lambda h, b, s: (b, _ATT_HEADS + h)),
            pl.BlockSpec((nblk, dh), lambda h, b, s: (b, h)),
        ],
        out_specs=pl.BlockSpec((group * blk, dh), lambda h, b, s: (b * steps + s, h)),
        out_shape=jax.ShapeDtypeStruct((t, d), _BF16),
        scratch_shapes=[
            pltpu.VMEM((n_tiles + 1, blk, blk), _F32),
            pltpu.VMEM((dh + 16, seq), _BF16),
        ],
        compiler_params=pltpu.CompilerParams(
            dimension_semantics=("arbitrary", "arbitrary", "arbitrary"),
            vmem_limit_bytes=_vmem_limit(block_bytes, scratch_bytes, 16 << 20)),
        name="moba_attention",
    )(rel_bias.astype(_F32), q, kv, kv, k_means)


def _rotary_tables(seq, half):
    inv = _ROPE_BASE ** (-jnp.arange(half, dtype=_F32) / half)
    ang = jnp.arange(seq).astype(_F32)[:, None] * inv[None, :]
    return jnp.cos(ang), jnp.sin(ang)


def kernel(x, p, ret_norm_g, ret_w_in, ret_w_out, kv_norm_g, w_kv, att_norm_g, att_w_q, att_w_o,
           rel_bias, mlp_norm_g, mlp_w_up, mlp_w_down, ple_norm_g, ple_w_up, ple_w_gate,
           final_norm_g):
    batch, seq, d = x.shape
    depth = p.shape[0]
    n_ret = ret_w_in.shape[0]
    t = batch * seq
    h = x.reshape(t, d)
    p2 = p.reshape(depth, t, p.shape[-1])
    cos, sin = _rotary_tables(seq, (d // _RET_HEADS) // 2)
    kv = k_means = None
    for i in range(depth):
        if i < n_ret:
            w_in = ret_w_in[i].astype(_BF16)
            qk = _retention_qk(h, ret_norm_g[i], w_in, cos, sin, seq)
            vg = _retention_vg(h, ret_norm_g[i], w_in)
            y = _retention_core(qk, vg, batch, seq, d)
            h = _matmul_residual(y, ret_w_out[i].astype(_BF16), h, bm=1024, bn=1024, name="ret_out")
        else:
            j = i - n_ret
            q = _norm_matmul(h, att_norm_g[j], att_w_q[j].astype(_BF16), bm=1024, bn=1024,
                             act=_identity, name="att_q")
            o = _moba_attention(q, kv, k_means, rel_bias, batch, seq)
            h = _matmul_residual(o, att_w_o[j].astype(_BF16), h, bm=1024, bn=1024, name="att_out")
        h = _mlp(h, mlp_norm_g[i], mlp_w_up[i].astype(_BF16), mlp_w_down[i].astype(_BF16),
                 name=f"mlp{i}")
        h = _per_layer_embed(h, p2[i], ple_norm_g[i], ple_w_up[i].astype(_BF16),
                             ple_w_gate[i].astype(_BF16), name=f"ple{i}",
                             final_gain=final_norm_g if i == depth - 1 else None)
        if i == n_ret - 1:
            kv, k_means = _shared_kv(h, kv_norm_g, w_kv.astype(_BF16))
    return h.reshape(batch, seq, d)
```
